```python
import jax, jax.numpy as jnp
from jax import lax
import numpy as np

D_MODEL = 2048
BATCH = 4
SEQ = 4096
DEPTH = 1

GRID_W = 64
CTX_LEN = 256
D_FF = 5632
MACARON_WEIGHT = 0.5
CONV_W = 1024
GLA_HEADS = 4
GLA_DK = 1024
GLA_DV = 2048
HEAD_K = GLA_DK // GLA_HEADS
HEAD_V = GLA_DV // GLA_HEADS
GATE_RANK = 16
GATE_TAU = 16.0
CHUNK = 64
N_MOD = 9
EPS = 1e-6
PROJ_SIZES = (CONV_W, CONV_W, CONV_W, GLA_DK, GLA_DK, GLA_DV, GLA_DV, GATE_RANK, GATE_RANK, D_MODEL, D_MODEL)
D_IN = 3 * CONV_W + 2 * GLA_DK + 2 * GLA_DV + 2 * GATE_RANK + 2 * D_MODEL

kernel_name = "hybrid_conv_gla_macaron_prefix_block"


def rmsnorm(x, g):
    xf = x.astype(jnp.float32)
    y = xf * lax.rsqrt(jnp.mean(xf * xf, axis=-1, keepdims=True) + EPS)
    return (y * g.astype(jnp.float32)).astype(x.dtype)


def modulate(u, shift, scale):
    return u * (1 + scale) + shift


def ffn_sublayer(h, shift, scale, gate, g_pre, g_post, w_in, w_out):
    u = modulate(rmsnorm(h, g_pre), shift, scale)
    a, b = jnp.split(u @ w_in, 2, axis=-1)
    y = (jax.nn.silu(b) * a) @ w_out
    return h + MACARON_WEIGHT * gate * rmsnorm(y, g_post)


def split_projection(u, w_in):
    points = np.cumsum(np.array(PROJ_SIZES))[:-1].tolist()
    return jnp.split(u @ w_in, points, axis=-1)


def conv3(z, w, b):
    zp = jnp.pad(z, [(0, 0)] * (z.ndim - 2) + [(1, 1), (0, 0)])
    return zp[..., :-2, :] * w[0] + zp[..., 1:-1, :] * w[1] + zp[..., 2:, :] * w[2] + b


def conv_branch(bg, cg, hv, w, b, w_out, on_grid):
    z = cg * hv
    if on_grid:
        bsz, length, ch = z.shape
        rows = length // GRID_W
        y = conv3(z.reshape(bsz, rows, GRID_W, ch), w, b).reshape(bsz, length, ch)
    else:
        y = conv3(z, w, b)
    return (bg * y) @ w_out


def to_heads(t):
    bsz, length, width = t.shape
    return t.reshape(bsz, length, GLA_HEADS, width // GLA_HEADS).transpose(0, 2, 1, 3).astype(jnp.float32)


def flip(t):
    return jnp.flip(t, axis=2)


def log_decay(lo, w_up, bias):
    return jax.nn.log_sigmoid((lo @ w_up + bias).astype(jnp.float32)) / GATE_TAU


def gla_prepare(q, k, v, lo_f, lo_b, gate_up, gate_bias):
    qh = to_heads(q) * HEAD_K ** -0.5
    kh = to_heads(k)
    vh = to_heads(v)
    la_f = to_heads(log_decay(lo_f, gate_up[0], gate_bias[0]))
    la_b = to_heads(log_decay(lo_b, gate_up[1], gate_bias[1]))
    return qh, kh, vh, la_f, la_b


def gla_chunked(q, k, v, log_a, s0):
    bsz, nh, length, dk = q.shape
    dv = v.shape[-1]
    n = length // CHUNK
    rs = lambda t: t.reshape(bsz, nh, n, CHUNK, t.shape[-1])
    q, k, v, log_a = rs(q), rs(k), rs(v), rs(log_a)
    b = jnp.cumsum(log_a, axis=-2)
    b_mid = b[..., CHUNK // 2 - 1:CHUNK // 2, :]
    b_last = b[..., -1:, :]
    scores = jnp.einsum('bhnck,bhnsk->bhncs', q * jnp.exp(b - b_mid), k * jnp.exp(b_mid - b))
    mask = jnp.tril(jnp.ones((CHUNK, CHUNK), dtype=bool))
    o_intra = jnp.einsum('bhncs,bhnsv->bhncv', jnp.where(mask, scores, 0.0), v)
    q_dec = q * jnp.exp(b)
    k_dec = k * jnp.exp(b_last - b)
    a_last = jnp.exp(b_last[..., 0, :])

    def step(s, xs):
        qd, kd, vc, al = xs
        o = jnp.einsum('bhck,bhkv->bhcv', qd, s)
        s = al[..., None] * s + jnp.einsum('bhck,bhcv->bhkv', kd, vc)
        return s, o

    mv = lambda t: jnp.moveaxis(t, 2, 0)
    s_final, o_inter = lax.scan(step, s0, (mv(q_dec), mv(k_dec), mv(v), mv(a_last)))
    o = o_intra + jnp.moveaxis(o_inter, 0, 2)
    return o.reshape(bsz, nh, length, dv), s_final


def gla_final_state(k, v, log_a):
    b = jnp.cumsum(log_a, axis=-2)
    k_dec = k * jnp.exp(b[..., -1:, :] - b)
    return jnp.einsum('bhlk,bhlv->bhkv', k_dec, v)


def gla_readout(o, r, g, w_out):
    on = o * lax.rsqrt(jnp.mean(o * o, axis=-1, keepdims=True) + EPS)
    bsz, nh, length, dv = o.shape
    on = on.transpose(0, 2, 1, 3).reshape(bsz, length, nh * dv).astype(r.dtype)
    return (on * g * jax.nn.silu(r)) @ w_out


def merge_branches(y_a, y_b, g_a, g_b, w_o):
    return (jax.nn.sigmoid(g_a) * y_a + jax.nn.sigmoid(g_b) * y_b) @ w_o


def setup_inputs(seed: int = 0) -> dict:
    key = jax.random.key(seed)
    ks = jax.random.split(key, 24)
    nrm = lambda k, shape: jax.random.normal(k, shape, jnp.float32)
    L = DEPTH
    return {
        "x": nrm(ks[0], (BATCH, SEQ, D_MODEL)),
        "c": nrm(ks[1], (BATCH, D_MODEL)),
        "ctx": nrm(ks[2], (BATCH, CTX_LEN, D_MODEL)),
        "c_ctx": nrm(ks[3], (D_MODEL,)),
        "w_mod": nrm(ks[4], (L, D_MODEL, N_MOD * D_MODEL)) * (0.5 * D_MODEL ** -0.5),
        "b_mod": nrm(ks[5], (L, N_MOD * D_MODEL)) * 0.02,
        "norm_g": 1.0 + 0.05 * nrm(ks[6], (L, 6, D_MODEL)),
        "ffn1_w_in": nrm(ks[7], (L, D_MODEL, 2 * D_FF)) * D_MODEL ** -0.5,
        "ffn1_w_out": nrm(ks[8], (L, D_FF, D_MODEL)) * D_FF ** -0.5,
        "w_in": nrm(ks[9], (L, D_MODEL, D_IN)) * D_MODEL ** -0.5,
        "conv_w": nrm(ks[10], (L, 3, CONV_W)) * 3.0 ** -0.5,
        "conv_b": nrm(ks[11], (L, CONV_W)) * 0.02,
        "conv_out": nrm(ks[12], (L, CONV_W, D_MODEL)) * CONV_W ** -0.5,
        "gate_up": nrm(ks[13], (L, 2, GATE_RANK, GLA_DK)) * GATE_RANK ** -0.5,
        "gate_bias": nrm(ks[14], (L, 2, GLA_DK)) * 0.1,
        "gla_norm_g": 1.0 + 0.05 * nrm(ks[15], (L, GLA_DV)),
        "gla_out": nrm(ks[16], (L, GLA_DV, D_MODEL)) * GLA_DV ** -0.5,
        "w_o": nrm(ks[17], (L, D_MODEL, D_MODEL)) * D_MODEL ** -0.5,
        "ffn2_w_in": nrm(ks[18], (L, D_MODEL, 2 * D_FF)) * D_MODEL ** -0.5,
        "ffn2_w_out": nrm(ks[19], (L, D_FF, D_MODEL)) * D_FF ** -0.5,
    }


def reference(x, c, ctx, c_ctx, w_mod, b_mod, norm_g, ffn1_w_in, ffn1_w_out, w_in, conv_w, conv_b,
              conv_out, gate_up, gate_bias, gla_norm_g, gla_out, w_o, ffn2_w_in, ffn2_w_out):
    for l in range(DEPTH):
        last = l == DEPTH - 1
        g = norm_g[l]
        m_x = [m[:, None, :] for m in jnp.split(jax.nn.silu(c) @ w_mod[l] + b_mod[l], N_MOD, axis=-1)]
        m_c = jnp.split(jax.nn.silu(c_ctx) @ w_mod[l] + b_mod[l], N_MOD, axis=-1)

        x = ffn_sublayer(x, m_x[0], m_x[1], m_x[2], g[0], g[1], ffn1_w_in[l], ffn1_w_out[l])
        ctx = ffn_sublayer(ctx, m_c[0], m_c[1], m_c[2], g[0], g[1], ffn1_w_in[l], ffn1_w_out[l])

        bg, cg, hv, q, k, v, r, lo_f, lo_b, g_a, g_b = split_projection(
            modulate(rmsnorm(x, g[2]), m_x[3], m_x[4]), w_in[l])
        cbg, ccg, chv, cq, ck, cv, cr, clo_f, clo_b, cg_a, cg_b = split_projection(
            modulate(rmsnorm(ctx, g[2]), m_c[3], m_c[4]), w_in[l])
        qh, kh, vh, la_f, la_b = gla_prepare(q, k, v, lo_f, lo_b, gate_up[l], gate_bias[l])
        cqh, ckh, cvh, cla_f, cla_b = gla_prepare(cq, ck, cv, clo_f, clo_b, gate_up[l], gate_bias[l])

        if last:
            s_f = gla_final_state(ckh, cvh, cla_f)
            s_b = gla_final_state(flip(ckh), flip(cvh), flip(cla_b))
        else:
            zero = jnp.zeros(ckh.shape[:2] + (HEAD_K, HEAD_V), jnp.float32)
            co_f, s_f = gla_chunked(cqh, ckh, cvh, cla_f, zero)
            co_b, s_b = gla_chunked(flip(cqh), flip(ckh), flip(cvh), flip(cla_b), zero)
            cy = merge_branches(
                conv_branch(cbg, ccg, chv, conv_w[l], conv_b[l], conv_out[l], False),
                gla_readout(co_f + flip(co_b), cr, gla_norm_g[l], gla_out[l]),
                cg_a, cg_b, w_o[l])
            ctx = ctx + m_c[5] * rmsnorm(cy, g[3])
            ctx = ffn_sublayer(ctx, m_c[6], m_c[7], m_c[8], g[4], g[5], ffn2_w_in[l], ffn2_w_out[l])

        o_f, _ = gla_chunked(qh, kh, vh, la_f, s_f)
        o_b, _ = gla_chunked(flip(qh), flip(kh), flip(vh), flip(la_b), s_b)
        y = merge_branches(
            conv_branch(bg, cg, hv, conv_w[l], conv_b[l], conv_out[l], True),
            gla_readout(o_f + flip(o_b), r, gla_norm_g[l], gla_out[l]),
            g_a, g_b, w_o[l])
        x = x + m_x[5] * rmsnorm(y, g[3])

        x = ffn_sublayer(x, m_x[6], m_x[7], m_x[8], g[4], g[5], ffn2_w_in[l], ffn2_w_out[l])
    return x
```

```python
import functools

import jax
import jax.numpy as jnp
from jax import lax
from jax.experimental import pallas as pl
from jax.experimental.pallas import tpu as pltpu

F32 = jnp.float32
BF16 = jnp.bfloat16

D_MODEL = 2048
BATCH = 4
SEQ = 4096
GRID_W = 64
CTX_LEN = 256
D_FF = 5632
MACARON_WEIGHT = 0.5
CONV_W = 1024
GLA_HEADS = 4
GLA_DK = 1024
GLA_DV = 2048
HEAD_K = GLA_DK // GLA_HEADS
HEAD_V = GLA_DV // GLA_HEADS
GATE_RANK = 16
GATE_TAU = 16.0
N_MOD = 9
EPS = 1e-6

LANES = 128
MOD_ROWS = 8
CTX_ROW = BATCH
GLA_CHUNK = 64
LO_PAD = LANES

COL_R, COL_GA, COL_GB, COL_V = 0, 2048, 4096, 6144
COL_BG, COL_CG, COL_HV, COL_Q, COL_K = 8192, 9216, 10240, 11264, 12288
P_COLS = 13312

VMEM_LIMIT = 56 * 1024 * 1024


def _dot(a, b):
    return jnp.dot(a, b, preferred_element_type=F32)


def _dot_t_lhs(a, b):
    return lax.dot_general(a, b, (((0,), (0,)), ((), ())), preferred_element_type=F32)


def _dot_t_rhs(a, b):
    return lax.dot_general(a, b, (((1,), (1,)), ((), ())), preferred_element_type=F32)


def _rms(x):
    return x * lax.rsqrt(jnp.mean(x * x, axis=-1, keepdims=True) + EPS)


def _silu(x):
    return x * jax.nn.sigmoid(x)


def _mod_kernel(c_ref, w_ref, b_ref, o_ref):
    s = _silu(c_ref[...]).astype(BF16)
    o_ref[...] = _dot(s, w_ref[...].astype(BF16)) + b_ref[...]


def _modulation(cc, w_mod, b_mod):
    n = w_mod.shape[1]
    tn = 1024
    return pl.pallas_call(
        _mod_kernel,
        grid=(n // tn,),
        in_specs=[
            pl.BlockSpec((MOD_ROWS, D_MODEL), lambda j: (0, 0)),
            pl.BlockSpec((D_MODEL, tn), lambda j: (0, j)),
            pl.BlockSpec((1, tn), lambda j: (0, j)),
        ],
        out_specs=pl.BlockSpec((MOD_ROWS, tn), lambda j: (0, j)),
        out_shape=jax.ShapeDtypeStruct((MOD_ROWS, n), F32),
        compiler_params=pltpu.CompilerParams(
            dimension_semantics=("arbitrary",), vmem_limit_bytes=VMEM_LIMIT),
        name="modulation",
    )(cc, w_mod, b_mod)


def _ffn_kernel(h_ref, m_ref, g_ref, wa_ref, wb_ref, wo_ref, o_ref, u_scr, acc_scr, *, mod0, gidx):
    j = pl.program_id(1)

    @pl.when(j == 0)
    def _():
        h = h_ref[...]
        shift = m_ref[0, mod0:mod0 + 1, :]
        scale = m_ref[0, mod0 + 1:mod0 + 2, :]
        u = _rms(h) * g_ref[gidx:gidx + 1, :]
        u_scr[...] = (u * (1.0 + scale) + shift).astype(BF16)
        acc_scr[...] = jnp.zeros_like(acc_scr)

    u = u_scr[...]
    a = _dot(u, wa_ref[...])
    b = _dot(u, wb_ref[...])
    act = (_silu(b) * a).astype(BF16)
    acc_scr[...] += _dot(act, wo_ref[...])

    @pl.when(j == pl.num_programs(1) - 1)
    def _():
        gate = m_ref[0, mod0 + 2:mod0 + 3, :]
        y = _rms(acc_scr[...]) * g_ref[gidx + 1:gidx + 2, :]
        o_ref[...] = h_ref[...] + MACARON_WEIGHT * gate * y


def _ffn(h, m3, norm_g, w_in, w_out, *, mod0, gidx, tm, tf, rows_per_mod, fixed_row=None):
    t = h.shape[0]
    nf = D_FF // tf
    tiles_per_mod = rows_per_mod // tm
    if fixed_row is None:
        m_map = lambda i, j: (i // tiles_per_mod, 0, 0)
    else:
        m_map = lambda i, j: (fixed_row, 0, 0)
    return pl.pallas_call(
        functools.partial(_ffn_kernel, mod0=mod0, gidx=gidx),
        grid=(t // tm, nf),
        in_specs=[
            pl.BlockSpec((tm, D_MODEL), lambda i, j: (i, 0)),
            pl.BlockSpec((1, N_MOD, D_MODEL), m_map),
            pl.BlockSpec((6, D_MODEL), lambda i, j: (0, 0)),
            pl.BlockSpec((D_MODEL, tf), lambda i, j: (0, j)),
            pl.BlockSpec((D_MODEL, tf), lambda i, j: (0, j + nf)),
            pl.BlockSpec((tf, D_MODEL), lambda i, j: (j, 0)),
        ],
        out_specs=pl.BlockSpec((tm, D_MODEL), lambda i, j: (i, 0)),
        out_shape=jax.ShapeDtypeStruct((t, D_MODEL), F32),
        scratch_shapes=[pltpu.VMEM((tm, D_MODEL), BF16), pltpu.VMEM((tm, D_MODEL), F32)],
        compiler_params=pltpu.CompilerParams(
            dimension_semantics=("parallel", "arbitrary"), vmem_limit_bytes=VMEM_LIMIT),
        name="ffn",
    )(h, m3, norm_g, w_in, w_in, w_out)


def _proj_kernel(x_ref, m_ref, g_ref, w_ref, wlo_ref, p_ref, lo_ref, u_scr):
    n = pl.program_id(1)

    @pl.when(n == 0)
    def _():
        shift = m_ref[0, 3:4, :]
        scale = m_ref[0, 4:5, :]
        u = _rms(x_ref[...]) * g_ref[2:3, :]
        ub = (u * (1.0 + scale) + shift).astype(BF16)
        u_scr[...] = ub
        lo_ref[...] = _dot(ub, wlo_ref[...])

    p_ref[...] = _dot(u_scr[...], w_ref[...]).astype(BF16)


def _proj(x, m3, norm_g, w, w_lo, *, tm, tn, rows_per_mod, fixed_row=None):
    t = x.shape[0]
    ncol = w.shape[1]
    tiles_per_mod = rows_per_mod // tm
    if fixed_row is None:
        m_map = lambda i, n: (i // tiles_per_mod, 0, 0)
    else:
        m_map = lambda i, n: (fixed_row, 0, 0)
    return pl.pallas_call(
        _proj_kernel,
        grid=(t // tm, ncol // tn),
        in_specs=[
            pl.BlockSpec((tm, D_MODEL), lambda i, n: (i, 0)),
            pl.BlockSpec((1, N_MOD, D_MODEL), m_map),
            pl.BlockSpec((6, D_MODEL), lambda i, n: (0, 0)),
            pl.BlockSpec((D_MODEL, tn), lambda i, n: (0, n)),
            pl.BlockSpec((D_MODEL, LO_PAD), lambda i, n: (0, 0)),
        ],
        out_specs=[
            pl.BlockSpec((tm, tn), lambda i, n: (i, n)),
            pl.BlockSpec((tm, LO_PAD), lambda i, n: (i, 0)),
        ],
        out_shape=[
            jax.ShapeDtypeStruct((t, ncol), BF16),
            jax.ShapeDtypeStruct((t, LO_PAD), F32),
        ],
        scratch_shapes=[pltpu.VMEM((tm, D_MODEL), BF16)],
        compiler_params=pltpu.CompilerParams(
            dimension_semantics=("parallel", "arbitrary"), vmem_limit_bytes=VMEM_LIMIT),
        name="proj",
    )(x, m3, norm_g, w, w_lo)


def _log_decay(lo_bf16, gu, bias):
    z = _dot(lo_bf16, gu) + bias
    return -(jnp.maximum(-z, 0.0) + jnp.log1p(jnp.exp(-jnp.abs(z)))) * (1.0 / GATE_TAU)


def _split3(x):
    hi = x.astype(BF16)
    r1 = x - hi.astype(F32)
    mid = r1.astype(BF16)
    lo = (r1 - mid.astype(F32)).astype(BF16)
    return jnp.concatenate([hi, mid, lo], axis=0)


def _tri3(n, upper, strict):
    r = lax.broadcasted_iota(jnp.int32, (n, 3 * n), 0)
    c = lax.broadcasted_iota(jnp.int32, (n, 3 * n), 1) & (n - 1)
    if upper:
        m = (c > r) if strict else (c >= r)
    else:
        m = (c < r) if strict else (c <= r)
    return m.astype(BF16)


def _gla_kernel(q_ref, k_ref, v_ref, lo_ref, ck_ref, cv_ref, clo_ref, gu_ref, gb_ref,
                on_ref, obuf, sf_ref, sb_ref):
    C = GLA_CHUNK
    n_chunks = SEQ // C
    gu_f = gu_ref[0]
    gu_b = gu_ref[1]
    bias_f = gb_ref[0]
    bias_b = gb_ref[1]

    clo = clo_ref[0].astype(BF16)
    ck = ck_ref[0].astype(F32)
    cv = cv_ref[0]
    la_cf = _log_decay(clo, gu_f, bias_f)
    la_cb = _log_decay(clo, gu_b, bias_b)
    dec_f = jnp.exp(_dot(_tri3(CTX_LEN, True, True), _split3(la_cf)))
    dec_b = jnp.exp(_dot(_tri3(CTX_LEN, False, True), _split3(la_cb)))
    sf_ref[...] = _dot_t_lhs((ck * dec_f).astype(BF16), cv)
    sb_ref[...] = _dot_t_lhs((ck * dec_b).astype(BF16), cv)

    tri_f = _tri3(C, False, False)
    tri_b = _tri3(C, True, False)
    ones3 = jnp.ones((3 * C, LANES), BF16)
    row = lax.broadcasted_iota(jnp.int32, (C, C), 0)
    col = lax.broadcasted_iota(jnp.int32, (C, C), 1)
    mask_f = col <= row
    mask_b = col >= row

    def chunk(r0, s_ref, fwd):
        qc = q_ref[0, pl.ds(r0, C), :].astype(F32) * (HEAD_K ** -0.5)
        kc = k_ref[0, pl.ds(r0, C), :].astype(F32)
        vc = v_ref[0, pl.ds(r0, C), :]
        loc = lo_ref[0, pl.ds(r0, C), :].astype(BF16)
        la = _log_decay(loc, gu_f if fwd else gu_b, bias_f if fwd else bias_b)
        la3 = _split3(la)
        b = _dot(tri_f if fwd else tri_b, la3)
        if fwd:
            b_mid, b_last = b[C // 2 - 1:C // 2, :], b[C - 1:C, :]
        else:
            b_mid, b_last = b[C // 2:C // 2 + 1, :], b[0:1, :]
        qs = qc * jnp.exp(b - b_mid)
        ks = kc * jnp.exp(b_mid - b)
        sc = _dot_t_rhs(qs.astype(BF16), ks.astype(BF16))
        sc = jnp.where(mask_f if fwd else mask_b, sc, 0.0)
        qd = (qs * jnp.exp(b_mid)).astype(BF16)
        kd = (ks * jnp.exp(b_last - b_mid)).astype(BF16)
        s = s_ref[...]
        o = _dot(sc.astype(BF16), vc) + _dot(qd, s.astype(BF16))
        a_col = jnp.exp(_dot_t_lhs(la3, ones3))
        a_full = jnp.concatenate([a_col] * (HEAD_V // LANES), axis=1)
        s_ref[...] = s * a_full + _dot_t_lhs(kd, vc)
        return o

    def first_half(i, carry):
        rf = pl.multiple_of(i * C, C)
        rb = pl.multiple_of((n_chunks - 1 - i) * C, C)
        obuf[pl.ds(rf, C), :] = chunk(rf, sf_ref, True)
        obuf[pl.ds(rb, C), :] = chunk(rb, sb_ref, False)
        return carry

    def finish(r0, o):
        o = o + obuf[pl.ds(r0, C), :]
        on_ref[0, pl.ds(r0, C), :] = _rms(o).astype(BF16)

    def second_half(i, carry):
        rf = pl.multiple_of(i * C, C)
        rb = pl.multiple_of((n_chunks - 1 - i) * C, C)
        finish(rf, chunk(rf, sf_ref, True))
        finish(rb, chunk(rb, sb_ref, False))
        return carry

    lax.fori_loop(0, n_chunks // 2, first_half, 0)
    lax.fori_loop(n_chunks // 2, n_chunks, second_half, 0)


def _gla(p3, lo3, cp3, clo3, gu, gbias):
    qb, kb, vb = COL_Q // HEAD_K, COL_K // HEAD_K, COL_V // HEAD_V
    return pl.pallas_call(
        _gla_kernel,
        grid=(BATCH, GLA_HEADS),
        in_specs=[
            pl.BlockSpec((1, SEQ, HEAD_K), lambda b, h: (b, 0, qb + h)),
            pl.BlockSpec((1, SEQ, HEAD_K), lambda b, h: (b, 0, kb + h)),
            pl.BlockSpec((1, SEQ, HEAD_V), lambda b, h: (b, 0, vb + h)),
            pl.BlockSpec((1, SEQ, LO_PAD), lambda b, h: (b, 0, 0)),
            pl.BlockSpec((1, CTX_LEN, HEAD_K), lambda b, h: (b, 0, h)),
            pl.BlockSpec((1, CTX_LEN, HEAD_V), lambda b, h: (b, 0, GLA_DK // HEAD_V + h)),
            pl.BlockSpec((1, CTX_LEN, LO_PAD), lambda b, h: (b, 0, 0)),
            pl.BlockSpec((2, LO_PAD, HEAD_K), lambda b, h: (0, 0, h)),
            pl.BlockSpec((2, 1, HEAD_K), lambda b, h: (0, 0, h)),
        ],
        out_specs=pl.BlockSpec((1, SEQ, HEAD_V), lambda b, h: (b, 0, h)),
        out_shape=jax.ShapeDtypeStruct((BATCH, SEQ, GLA_DV), BF16),
        scratch_shapes=[
            pltpu.VMEM((SEQ, HEAD_V), F32),
            pltpu.VMEM((HEAD_K, HEAD_V), F32),
            pltpu.VMEM((HEAD_K, HEAD_V), F32),
        ],
        compiler_params=pltpu.CompilerParams(
            dimension_semantics=("parallel", "arbitrary"), vmem_limit_bytes=VMEM_LIMIT),
        name="gla",
    )(p3, p3, p3, lo3, cp3, cp3, clo3, gu, gbias)


def _merge_kernel(bg_ref, cg_ref, hv_ref, r_ref, ga_ref, gb_ref, on_ref, x_ref, m_ref, g_ref,
                  cw_ref, cb_ref, gng_ref, wc_ref, wg_ref, wo_ref, o_ref):
    tm = x_ref.shape[0]
    z = cg_ref[...].astype(F32) * hv_ref[...].astype(F32)
    col = lax.broadcasted_iota(jnp.int32, (tm, 1), 0) & (GRID_W - 1)
    z_prev = jnp.where(col == 0, 0.0, pltpu.roll(z, 1, 0))
    z_next = jnp.where(col == GRID_W - 1, 0.0, pltpu.roll(z, tm - 1, 0))
    y = z_prev * cw_ref[0:1, :] + z * cw_ref[1:2, :] + z_next * cw_ref[2:3, :] + cb_ref[...]
    y_a = _dot((bg_ref[...].astype(F32) * y).astype(BF16), wc_ref[...])
    rd = on_ref[...].astype(F32) * gng_ref[...] * _silu(r_ref[...].astype(F32))
    y_b = _dot(rd.astype(BF16), wg_ref[...])
    mix = (jax.nn.sigmoid(ga_ref[...].astype(F32)) * y_a
           + jax.nn.sigmoid(gb_ref[...].astype(F32)) * y_b)
    yo = _dot(mix.astype(BF16), wo_ref[...])
    gate = m_ref[0, 5:6, :]
    o_ref[...] = x_ref[...] + gate * (_rms(yo) * g_ref[3:4, :])


def _merge(p, on, x1, m3, norm_g, conv_w, conv_b, gng, wc, wg, wo, *, tm):
    t = x1.shape[0]
    tiles_per_mod = SEQ // tm
    tok = lambda blk: (lambda i: (i, blk))
    const2 = lambda i: (0, 0)
    return pl.pallas_call(
        _merge_kernel,
        grid=(t // tm,),
        in_specs=[
            pl.BlockSpec((tm, CONV_W), tok(COL_BG // CONV_W)),
            pl.BlockSpec((tm, CONV_W), tok(COL_CG // CONV_W)),
            pl.BlockSpec((tm, CONV_W), tok(COL_HV // CONV_W)),
            pl.BlockSpec((tm, GLA_DV), tok(COL_R // GLA_DV)),
            pl.BlockSpec((tm, D_MODEL), tok(COL_GA // D_MODEL)),
            pl.BlockSpec((tm, D_MODEL), tok(COL_GB // D_MODEL)),
            pl.BlockSpec((tm, GLA_DV), tok(0)),
            pl.BlockSpec((tm, D_MODEL), tok(0)),
            pl.BlockSpec((1, N_MOD, D_MODEL), lambda i: (i // tiles_per_mod, 0, 0)),
            pl.BlockSpec((6, D_MODEL), const2),
            pl.BlockSpec((3, CONV_W), const2),
            pl.BlockSpec((1, CONV_W), const2),
            pl.BlockSpec((1, GLA_DV), const2),
            pl.BlockSpec((CONV_W, D_MODEL), const2),
            pl.BlockSpec((GLA_DV, D_MODEL), const2),
            pl.BlockSpec((D_MODEL, D_MODEL), const2),
        ],
        out_specs=pl.BlockSpec((tm, D_MODEL), tok(0)),
        out_shape=jax.ShapeDtypeStruct((t, D_MODEL), F32),
        compiler_params=pltpu.CompilerParams(
            dimension_semantics=("parallel",), vmem_limit_bytes=VMEM_LIMIT),
        name="merge",
    )(p, p, p, p, p, p, on, x1, m3, norm_g, conv_w, conv_b, gng, wc, wg, wo)


def kernel(x, c, ctx, c_ctx, w_mod, b_mod, norm_g, ffn1_w_in, ffn1_w_out, w_in, conv_w, conv_b,
           conv_out, gate_up, gate_bias, gla_norm_g, gla_out, w_o, ffn2_w_in, ffn2_w_out):
    assert w_mod.shape[0] == 1, "single-layer block"
    t = BATCH * SEQ
    tc = BATCH * CTX_LEN

    w = w_in[0]
    pts = [0, 1024, 2048, 3072, 4096, 5120, 7168, 9216, 9232, 9248, 11296, 13344]
    seg = lambda a: w[:, pts[a]:pts[a + 1]]
    w_bg, w_cg, w_hv, w_q, w_k, w_v, w_r, w_lof, w_lob, w_ga, w_gb = [seg(a) for a in range(11)]
    w_main = jnp.concatenate([w_r, w_ga, w_gb, w_v, w_bg, w_cg, w_hv, w_q, w_k], axis=1).astype(BF16)
    w_ctx = jnp.concatenate([w_k, w_v], axis=1).astype(BF16)
    w_lo = jnp.concatenate(
        [w_lof, w_lob, jnp.zeros((D_MODEL, LO_PAD - 2 * GATE_RANK), F32)], axis=1).astype(BF16)
    gu = jnp.zeros((2, LO_PAD, GLA_DK), F32)
    gu = gu.at[0, 0:GATE_RANK].set(gate_up[0, 0]).at[1, GATE_RANK:2 * GATE_RANK].set(gate_up[0, 1])
    gu = gu.astype(BF16)
    gbias = gate_bias[0].reshape(2, 1, GLA_DK)
    f1_in, f1_out = ffn1_w_in[0].astype(BF16), ffn1_w_out[0].astype(BF16)
    f2_in, f2_out = ffn2_w_in[0].astype(BF16), ffn2_w_out[0].astype(BF16)
    wc, wg, wo = conv_out[0].astype(BF16), gla_out[0].astype(BF16), w_o[0].astype(BF16)
    g = norm_g[0]

    cc = jnp.concatenate([c, c_ctx[None, :], jnp.zeros((MOD_ROWS - BATCH - 1, D_MODEL), F32)], axis=0)
    m3 = _modulation(cc, w_mod[0], b_mod).reshape(MOD_ROWS, N_MOD, D_MODEL)

    xf = x.reshape(t, D_MODEL)
    cf = ctx.reshape(tc, D_MODEL)
    x1 = _ffn(xf, m3, g, f1_in, f1_out, mod0=0, gidx=0, tm=512, tf=512, rows_per_mod=SEQ)
    c1 = _ffn(cf, m3, g, f1_in, f1_out, mod0=0, gidx=0, tm=512, tf=512, rows_per_mod=CTX_LEN,
              fixed_row=CTX_ROW)

    p, lo = _proj(x1, m3, g, w_main, w_lo, tm=1024, tn=1024, rows_per_mod=SEQ)
    cp, clo = _proj(c1, m3, g, w_ctx, w_lo, tm=1024, tn=1024, rows_per_mod=CTX_LEN, fixed_row=CTX_ROW)

    on = _gla(p.reshape(BATCH, SEQ, P_COLS), lo.reshape(BATCH, SEQ, LO_PAD),
              cp.reshape(BATCH, CTX_LEN, GLA_DK + GLA_DV), clo.reshape(BATCH, CTX_LEN, LO_PAD),
              gu, gbias)

    x2 = _merge(p, on.reshape(t, GLA_DV), x1, m3, g, conv_w[0], conv_b, gla_norm_g, wc, wg, wo, tm=256)
    out = _ffn(x2, m3, g, f2_in, f2_out, mod0=6, gidx=4, tm=512, tf=512, rows_per_mod=SEQ)
    return out.reshape(BATCH, SEQ, D_MODEL)
```

```python
import functools

import jax
import jax.numpy as jnp
from jax import lax
from jax.experimental import pallas as pl
from jax.experimental.pallas import tpu as pltpu

F32 = jnp.float32
BF16 = jnp.bfloat16

D_MODEL = 2048
BATCH = 4
SEQ = 4096
GRID_W = 64
CTX_LEN = 256
D_FF = 5632
MACARON_WEIGHT = 0.5
CONV_W = 1024
GLA_HEADS = 4
GLA_DK = 1024
GLA_DV = 2048
HEAD_K = GLA_DK // GLA_HEADS
HEAD_V = GLA_DV // GLA_HEADS
GATE_RANK = 16
GATE_TAU = 16.0
N_MOD = 9
EPS = 1e-6
LOG2_E = 1.4426950408889634

LANES = 128
MOD_ROWS = 8
CTX_ROW = BATCH
GLA_CHUNK = 64
GLA_PREP_CHUNKS = 4
GLA_FINISH_ROWS = 256
LO_PAD = LANES

PROJ_TN = 1024
PROJ_HEAD_BLOCKS = 9
PROJ_BLOCKS = 13
PROJ_ROT = 8
PROJ_K_BLOCK = 4
COL_V, COL_R, COL_GA, COL_GB = 0, 2048, 4096, 6144
COL_BG, COL_CG, COL_HV, COL_Q, COL_K = 8192, 9216, 10240, 11264, 12288
P_COLS = PROJ_BLOCKS * PROJ_TN

VMEM_LIMIT = 56 * 1024 * 1024
GLA_VMEM_LIMIT = 60 * 1024 * 1024


def _dot(a, b):
    return jnp.dot(a, b, preferred_element_type=F32)


def _dot_t_lhs(a, b):
    return lax.dot_general(a, b, (((0,), (0,)), ((), ())), preferred_element_type=F32)


def _dot_t_rhs(a, b):
    return lax.dot_general(a, b, (((1,), (1,)), ((), ())), preferred_element_type=F32)


def _rms(x):
    return x * lax.rsqrt(jnp.mean(x * x, axis=-1, keepdims=True) + EPS)


def _silu(x):
    return x * jax.nn.sigmoid(x)


def _mod_kernel(c_ref, w_ref, b_ref, o_ref):
    s = _silu(c_ref[...]).astype(BF16)
    o_ref[...] = _dot(s, w_ref[...].astype(BF16)) + b_ref[...]


def _modulation(cc, w_mod, b_mod):
    n = w_mod.shape[1]
    tn = 1024
    return pl.pallas_call(
        _mod_kernel,
        grid=(n // tn,),
        in_specs=[
            pl.BlockSpec((MOD_ROWS, D_MODEL), lambda j: (0, 0)),
            pl.BlockSpec((D_MODEL, tn), lambda j: (0, j)),
            pl.BlockSpec((1, tn), lambda j: (0, j)),
        ],
        out_specs=pl.BlockSpec((MOD_ROWS, tn), lambda j: (0, j)),
        out_shape=jax.ShapeDtypeStruct((MOD_ROWS, n), F32),
        compiler_params=pltpu.CompilerParams(
            dimension_semantics=("arbitrary",), vmem_limit_bytes=VMEM_LIMIT),
        name="modulation",
    )(cc, w_mod, b_mod)


def _ffn_kernel(h_ref, m_ref, g_ref, wa_ref, wb_ref, wo_ref, o_ref, u_scr, acc_scr, *, mod0, gidx):
    j = pl.program_id(1)

    @pl.when(j == 0)
    def _():
        h = h_ref[...]
        shift = m_ref[0, mod0:mod0 + 1, :]
        scale = m_ref[0, mod0 + 1:mod0 + 2, :]
        u = _rms(h) * g_ref[gidx:gidx + 1, :]
        u_scr[...] = (u * (1.0 + scale) + shift).astype(BF16)
        acc_scr[...] = jnp.zeros_like(acc_scr)

    u = u_scr[...]
    a = _dot(u, wa_ref[...])
    b = _dot(u, wb_ref[...])
    act = (_silu(b) * a).astype(BF16)
    acc_scr[...] += _dot(act, wo_ref[...])

    @pl.when(j == pl.num_programs(1) - 1)
    def _():
        gate = m_ref[0, mod0 + 2:mod0 + 3, :]
        y = _rms(acc_scr[...]) * g_ref[gidx + 1:gidx + 2, :]
        o_ref[...] = h_ref[...] + MACARON_WEIGHT * gate * y


def _ffn(h, m3, norm_g, w_in, w_out, *, mod0, gidx, tm, tf, rows_per_mod, fixed_row=None):
    t = h.shape[0]
    nf = D_FF // tf
    tiles_per_mod = rows_per_mod // tm
    if fixed_row is None:
        m_map = lambda i, j: (i // tiles_per_mod, 0, 0)
    else:
        m_map = lambda i, j: (fixed_row, 0, 0)
    return pl.pallas_call(
        functools.partial(_ffn_kernel, mod0=mod0, gidx=gidx),
        grid=(t // tm, nf),
        in_specs=[
            pl.BlockSpec((tm, D_MODEL), lambda i, j: (i, 0)),
            pl.BlockSpec((1, N_MOD, D_MODEL), m_map),
            pl.BlockSpec((6, D_MODEL), lambda i, j: (0, 0)),
            pl.BlockSpec((D_MODEL, tf), lambda i, j: (0, j)),
            pl.BlockSpec((D_MODEL, tf), lambda i, j: (0, j + nf)),
            pl.BlockSpec((tf, D_MODEL), lambda i, j: (j, 0)),
        ],
        out_specs=pl.BlockSpec((tm, D_MODEL), lambda i, j: (i, 0)),
        out_shape=jax.ShapeDtypeStruct((t, D_MODEL), F32),
        scratch_shapes=[pltpu.VMEM((tm, D_MODEL), BF16), pltpu.VMEM((tm, D_MODEL), F32)],
        compiler_params=pltpu.CompilerParams(
            dimension_semantics=("parallel", "arbitrary"), vmem_limit_bytes=VMEM_LIMIT),
        name="ffn",
    )(h, m3, norm_g, w_in, w_in, w_out)


def _proj_kernel(x_ref, m_ref, g_ref, w_ref, wg_ref, wlo_ref, p_ref, lo_ref, u_scr, *, head_blocks):
    n = pl.program_id(1)

    @pl.when(n == 0)
    def _():
        shift = m_ref[0, 3:4, :]
        scale = m_ref[0, 4:5, :]
        u = _rms(x_ref[...]) * g_ref[2:3, :]
        ub = (u * (1.0 + scale) + shift).astype(BF16)
        u_scr[...] = ub
        lo_ref[...] = _dot(ub, wlo_ref[...]).astype(BF16)

    @pl.when(n < head_blocks)
    def _():
        p_ref[...] = _dot(u_scr[...], w_ref[...]).astype(BF16)

    @pl.when(n >= head_blocks)
    def _():
        p_ref[...] = _dot(u_scr[...], wg_ref[...]).astype(BF16)


def _proj(x, m3, norm_g, w, w_gates, w_lo, *, tm, first_block, head_blocks, n_blocks, rot,
          rows_per_mod, fixed_row=None):
    t = x.shape[0]
    tn = PROJ_TN
    tiles_per_mod = rows_per_mod // tm
    if fixed_row is None:
        m_map = lambda i, n: (i // tiles_per_mod, 0, 0)
    else:
        m_map = lambda i, n: (fixed_row, 0, 0)
    return pl.pallas_call(
        functools.partial(_proj_kernel, head_blocks=head_blocks),
        grid=(t // tm, n_blocks),
        in_specs=[
            pl.BlockSpec((tm, D_MODEL), lambda i, n: (i, 0)),
            pl.BlockSpec((1, N_MOD, D_MODEL), m_map),
            pl.BlockSpec((6, D_MODEL), lambda i, n: (0, 0)),
            pl.BlockSpec((D_MODEL, tn), lambda i, n: (0, first_block + jnp.minimum(n, head_blocks - 1))),
            pl.BlockSpec((D_MODEL, tn), lambda i, n: (0, jnp.maximum(n - head_blocks, 0))),
            pl.BlockSpec((D_MODEL, LO_PAD), lambda i, n: (0, 0)),
        ],
        out_specs=[
            pl.BlockSpec((tm, tn), lambda i, n: (i, (n + rot) % n_blocks)),
            pl.BlockSpec((tm, LO_PAD), lambda i, n: (i, 0)),
        ],
        out_shape=[
            jax.ShapeDtypeStruct((t, n_blocks * tn), BF16),
            jax.ShapeDtypeStruct((t, LO_PAD), BF16),
        ],
        scratch_shapes=[pltpu.VMEM((tm, D_MODEL), BF16)],
        compiler_params=pltpu.CompilerParams(
            dimension_semantics=("parallel", "arbitrary"), vmem_limit_bytes=VMEM_LIMIT),
        name="proj",
    )(x, m3, norm_g, w, w_gates, w_lo)


def _log2_decay(lo_bf16, gu, bias):
    z = _dot(lo_bf16, gu) + bias
    return (jnp.minimum(z, 0.0) - jnp.log(1.0 + jnp.exp(-jnp.abs(z)))) * (LOG2_E / GATE_TAU)


def _split2(x):
    hi = x.astype(BF16)
    lo = (x - hi.astype(F32)).astype(BF16)
    return jnp.concatenate([hi, lo], axis=0)


def _tri2(n, upper, strict):
    r = lax.broadcasted_iota(jnp.int32, (n, 2 * n), 0)
    c = lax.broadcasted_iota(jnp.int32, (n, 2 * n), 1) & (n - 1)
    if upper:
        m = (c > r) if strict else (c >= r)
    else:
        m = (c < r) if strict else (c <= r)
    return m.astype(BF16)


def _gla_kernel(q_ref, k_ref, v_ref, lo_ref, ck_ref, cv_ref, clo_ref, gu_ref, gb_ref, on_ref,
                qd_f, kd_f, qd_b, kd_b, o_acc, o_bwd, a_f, a_b, st_f, st_b):
    C = GLA_CHUNK
    n_chunks = SEQ // C
    gu = gu_ref[0]
    bias = gb_ref[0]

    la_c = _log2_decay(clo_ref[0], gu, bias)
    ck = ck_ref[0].astype(F32)
    cv = cv_ref[0]
    dec_f = jnp.exp2(_dot(_tri2(CTX_LEN, True, True), _split2(la_c[:, :HEAD_K])))
    dec_b = jnp.exp2(_dot(_tri2(CTX_LEN, False, True), _split2(la_c[:, HEAD_K:])))
    st_f[...] = _dot_t_lhs(cv, (ck * dec_f).astype(BF16))
    st_b[...] = _dot_t_lhs(cv, (ck * dec_b).astype(BF16))

    tr = lax.broadcasted_iota(jnp.int32, (2 * C, 2 * C), 0)
    tc = lax.broadcasted_iota(jnp.int32, (2 * C, 2 * C), 1) & (C - 1)
    tri_fb = (((tr < C) & (tc <= tr)) | ((tr >= C) & (tc >= tr - C))).astype(BF16)
    pr = lax.broadcasted_iota(jnp.int32, (C, 2 * C), 0)
    pc = lax.broadcasted_iota(jnp.int32, (C, 2 * C), 1)
    keep_f = pc <= pr
    keep_b = pc - C >= pr

    def prepare(i, carry):
        r0 = pl.multiple_of(i * (GLA_PREP_CHUNKS * C), GLA_PREP_CHUNKS * C)
        la = _log2_decay(lo_ref[0, pl.ds(r0, GLA_PREP_CHUNKS * C), :], gu, bias)
        cum = [_dot(tri_fb, _split2(la[j * C:(j + 1) * C, :])) for j in range(GLA_PREP_CHUNKS)]
        scores = []
        for j in range(GLA_PREP_CHUNKS):
            rows = pl.ds(r0 + j * C, C)
            qc = q_ref[0, rows, :].astype(F32) * (HEAD_K ** -0.5)
            kc = k_ref[0, rows, :].astype(F32)
            b_f = cum[j][:C, :HEAD_K]
            b_b = cum[j][C:, HEAD_K:]
            parts = []
            for fwd, b in ((True, b_f), (False, b_b)):
                if fwd:
                    b_mid, b_last = b[C // 2 - 1:C // 2, :], b[C - 1:C, :]
                else:
                    b_mid, b_last = b[C // 2:C // 2 + 1, :], b[0:1, :]
                qs = qc * jnp.exp2(b - b_mid)
                ks = kc * jnp.exp2(b_mid - b)
                (qd_f if fwd else qd_b)[rows, :] = (qs * jnp.exp2(b_mid)).astype(BF16)
                (kd_f if fwd else kd_b)[rows, :] = (ks * jnp.exp2(b_last - b_mid)).astype(BF16)
                (a_f if fwd else a_b)[pl.ds(i * GLA_PREP_CHUNKS + j, 1), :] = jnp.exp2(b_last)
                parts.append((qs.astype(BF16), ks.astype(BF16)))
            scores.append(_dot_t_rhs(jnp.concatenate([parts[0][0], parts[1][0]], axis=0),
                                     jnp.concatenate([parts[0][1], parts[1][1]], axis=0)))
        for j in range(GLA_PREP_CHUNKS):
            rows = pl.ds(r0 + j * C, C)
            vc = v_ref[0, rows, :]
            sc = jnp.where(keep_f, scores[j][:C, :], jnp.where(keep_b, scores[j][C:, :], 0.0))
            o_acc[rows, :] = _dot(sc.astype(BF16), jnp.concatenate([vc, vc], axis=0))
        return carry

    def scan(i, carry):
        steps = ((i, qd_f, kd_f, a_f, st_f), (n_chunks - 1 - i, qd_b, kd_b, a_b, st_b))
        rows = [pl.ds(pl.multiple_of(c * C, C), C) for c, *_ in steps]
        incs = [_dot_t_lhs(v_ref[0, r, :], kd[r, :]) for r, (_, _, kd, _, _) in zip(rows, steps)]
        outs = [_dot_t_rhs(qd[r, :], st[...].astype(BF16)) for r, (_, qd, _, _, st) in zip(rows, steps)]
        o_acc[rows[0], :] += outs[0]
        o_bwd[rows[1], :] = outs[1]
        for inc, (c, _, _, a, st) in zip(incs, steps):
            st[...] = st[...] * a[pl.ds(c, 1), :] + inc
        return carry

    def finish(i, carry):
        rows = pl.ds(pl.multiple_of(i * GLA_FINISH_ROWS, GLA_FINISH_ROWS), GLA_FINISH_ROWS)
        on_ref[0, rows, :] = _rms(o_acc[rows, :] + o_bwd[rows, :]).astype(BF16)
        return carry

    lax.fori_loop(0, n_chunks // GLA_PREP_CHUNKS, prepare, 0)
    lax.fori_loop(0, n_chunks, scan, 0, unroll=4)
    lax.fori_loop(0, SEQ // GLA_FINISH_ROWS, finish, 0)


def _gla(p3, lo3, cp3, clo3, gu, gbias):
    qb, kb, vb = COL_Q // HEAD_K, COL_K // HEAD_K, COL_V // HEAD_V
    n_chunks = SEQ // GLA_CHUNK
    return pl.pallas_call(
        _gla_kernel,
        grid=(BATCH, GLA_HEADS),
        in_specs=[
            pl.BlockSpec((1, SEQ, HEAD_K), lambda b, h: (b, 0, qb + h)),
            pl.BlockSpec((1, SEQ, HEAD_K), lambda b, h: (b, 0, kb + h)),
            pl.BlockSpec((1, SEQ, HEAD_V), lambda b, h: (b, 0, vb + h)),
            pl.BlockSpec((1, SEQ, LO_PAD), lambda b, h: (b, 0, 0)),
            pl.BlockSpec((1, CTX_LEN, HEAD_K), lambda b, h: (b, 0, h)),
            pl.BlockSpec((1, CTX_LEN, HEAD_V), lambda b, h: (b, 0, GLA_DK // HEAD_V + h)),
            pl.BlockSpec((1, CTX_LEN, LO_PAD), lambda b, h: (b, 0, 0)),
            pl.BlockSpec((1, LO_PAD, 2 * HEAD_K), lambda b, h: (h, 0, 0)),
            pl.BlockSpec((1, 1, 2 * HEAD_K), lambda b, h: (h, 0, 0)),
        ],
        out_specs=pl.BlockSpec((1, SEQ, HEAD_V), lambda b, h: (b, 0, h)),
        out_shape=jax.ShapeDtypeStruct((BATCH, SEQ, GLA_DV), BF16),
        scratch_shapes=[
            pltpu.VMEM((SEQ, HEAD_K), BF16),
            pltpu.VMEM((SEQ, HEAD_K), BF16),
            pltpu.VMEM((SEQ, HEAD_K), BF16),
            pltpu.VMEM((SEQ, HEAD_K), BF16),
            pltpu.VMEM((SEQ, HEAD_V), F32),
            pltpu.VMEM((SEQ, HEAD_V), F32),
            pltpu.VMEM((n_chunks, HEAD_K), F32),
            pltpu.VMEM((n_chunks, HEAD_K), F32),
            pltpu.VMEM((HEAD_V, HEAD_K), F32),
            pltpu.VMEM((HEAD_V, HEAD_K), F32),
        ],
        compiler_params=pltpu.CompilerParams(
            dimension_semantics=("parallel", "arbitrary"), vmem_limit_bytes=GLA_VMEM_LIMIT),
        name="gla",
    )(p3, p3, p3, lo3, cp3, cp3, clo3, gu, gbias)


def _merge_kernel(bg_ref, cg_ref, hv_ref, r_ref, ga_ref, gb_ref, on_ref, x_ref, m_ref, g_ref,
                  cw_ref, cb_ref, gng_ref, wc_ref, wg_ref, wo_ref, o_ref):
    tm = x_ref.shape[0]
    z = cg_ref[...].astype(F32) * hv_ref[...].astype(F32)
    col = lax.broadcasted_iota(jnp.int32, (tm, 1), 0) & (GRID_W - 1)
    z_prev = jnp.where(col == 0, 0.0, pltpu.roll(z, 1, 0))
    z_next = jnp.where(col == GRID_W - 1, 0.0, pltpu.roll(z, tm - 1, 0))
    y = z_prev * cw_ref[0:1, :] + z * cw_ref[1:2, :] + z_next * cw_ref[2:3, :] + cb_ref[...]
    y_a = _dot((bg_ref[...].astype(F32) * y).astype(BF16), wc_ref[...])
    rd = on_ref[...].astype(F32) * gng_ref[...] * _silu(r_ref[...].astype(F32))
    y_b = _dot(rd.astype(BF16), wg_ref[...])
    mix = (jax.nn.sigmoid(ga_ref[...].astype(F32)) * y_a
           + jax.nn.sigmoid(gb_ref[...].astype(F32)) * y_b)
    yo = _dot(mix.astype(BF16), wo_ref[...])
    gate = m_ref[0, 5:6, :]
    o_ref[...] = x_ref[...] + gate * (_rms(yo) * g_ref[3:4, :])


def _merge(p, on, x1, m3, norm_g, conv_w, conv_b, gng, wc, wg, wo, *, tm):
    t = x1.shape[0]
    tiles_per_mod = SEQ // tm
    tok = lambda blk: (lambda i: (i, blk))
    const2 = lambda i: (0, 0)
    return pl.pallas_call(
        _merge_kernel,
        grid=(t // tm,),
        in_specs=[
            pl.BlockSpec((tm, CONV_W), tok(COL_BG // CONV_W)),
            pl.BlockSpec((tm, CONV_W), tok(COL_CG // CONV_W)),
            pl.BlockSpec((tm, CONV_W), tok(COL_HV // CONV_W)),
            pl.BlockSpec((tm, GLA_DV), tok(COL_R // GLA_DV)),
            pl.BlockSpec((tm, D_MODEL), tok(COL_GA // D_MODEL)),
            pl.BlockSpec((tm, D_MODEL), tok(COL_GB // D_MODEL)),
            pl.BlockSpec((tm, GLA_DV), tok(0)),
            pl.BlockSpec((tm, D_MODEL), tok(0)),
            pl.BlockSpec((1, N_MOD, D_MODEL), lambda i: (i // tiles_per_mod, 0, 0)),
            pl.BlockSpec((6, D_MODEL), const2),
            pl.BlockSpec((3, CONV_W), const2),
            pl.BlockSpec((1, CONV_W), const2),
            pl.BlockSpec((1, GLA_DV), const2),
            pl.BlockSpec((CONV_W, D_MODEL), const2),
            pl.BlockSpec((GLA_DV, D_MODEL), const2),
            pl.BlockSpec((D_MODEL, D_MODEL), const2),
        ],
        out_specs=pl.BlockSpec((tm, D_MODEL), tok(0)),
        out_shape=jax.ShapeDtypeStruct((t, D_MODEL), F32),
        compiler_params=pltpu.CompilerParams(
            dimension_semantics=("parallel",), vmem_limit_bytes=VMEM_LIMIT),
        name="merge",
    )(p, p, p, p, p, p, on, x1, m3, norm_g, conv_w, conv_b, gng, wc, wg, wo)


def kernel(x, c, ctx, c_ctx, w_mod, b_mod, norm_g, ffn1_w_in, ffn1_w_out, w_in, conv_w, conv_b,
           conv_out, gate_up, gate_bias, gla_norm_g, gla_out, w_o, ffn2_w_in, ffn2_w_out):
    assert w_mod.shape[0] == 1, "single-layer block"
    t = BATCH * SEQ
    tc = BATCH * CTX_LEN

    w = w_in[0].astype(BF16)
    lo0 = PROJ_HEAD_BLOCKS * PROJ_TN
    w_gates = w[:, lo0 + 2 * GATE_RANK:]
    w_lo = jnp.pad(w[:, lo0:lo0 + 2 * GATE_RANK], ((0, 0), (0, LO_PAD - 2 * GATE_RANK)))
    per_head = lambda a: a.reshape(GATE_RANK, GLA_HEADS, HEAD_K).transpose(1, 0, 2)
    gu = jnp.zeros((GLA_HEADS, LO_PAD, 2 * HEAD_K), F32)
    gu = gu.at[:, 0:GATE_RANK, :HEAD_K].set(per_head(gate_up[0, 0]))
    gu = gu.at[:, GATE_RANK:2 * GATE_RANK, HEAD_K:].set(per_head(gate_up[0, 1]))
    gu = gu.astype(BF16)
    gbias = jnp.concatenate([gate_bias[0, 0].reshape(GLA_HEADS, 1, HEAD_K),
                             gate_bias[0, 1].reshape(GLA_HEADS, 1, HEAD_K)], axis=2)
    f1_in, f1_out = ffn1_w_in[0].astype(BF16), ffn1_w_out[0].astype(BF16)
    f2_in, f2_out = ffn2_w_in[0].astype(BF16), ffn2_w_out[0].astype(BF16)
    wc, wg, wo = conv_out[0].astype(BF16), gla_out[0].astype(BF16), w_o[0].astype(BF16)
    g = norm_g[0]

    cc = jnp.concatenate([c, c_ctx[None, :], jnp.zeros((MOD_ROWS - BATCH - 1, D_MODEL), F32)], axis=0)
    m3 = _modulation(cc, w_mod[0], b_mod).reshape(MOD_ROWS, N_MOD, D_MODEL)

    xf = x.reshape(t, D_MODEL)
    cf = ctx.reshape(tc, D_MODEL)
    x1 = _ffn(xf, m3, g, f1_in, f1_out, mod0=0, gidx=0, tm=512, tf=512, rows_per_mod=SEQ)
    c1 = _ffn(cf, m3, g, f1_in, f1_out, mod0=0, gidx=0, tm=512, tf=512, rows_per_mod=CTX_LEN,
              fixed_row=CTX_ROW)

    p, lo = _proj(x1, m3, g, w, w_gates, w_lo, tm=1024, first_block=0, head_blocks=PROJ_HEAD_BLOCKS,
                  n_blocks=PROJ_BLOCKS, rot=PROJ_ROT, rows_per_mod=SEQ)
    cp, clo = _proj(c1, m3, g, w, w_gates, w_lo, tm=1024, first_block=PROJ_K_BLOCK, head_blocks=3,
                    n_blocks=3, rot=0, rows_per_mod=CTX_LEN, fixed_row=CTX_ROW)

    on = _gla(p.reshape(BATCH, SEQ, P_COLS), lo.reshape(BATCH, SEQ, LO_PAD),
              cp.reshape(BATCH, CTX_LEN, GLA_DK + GLA_DV), clo.reshape(BATCH, CTX_LEN, LO_PAD),
              gu, gbias)

    x2 = _merge(p, on.reshape(t, GLA_DV), x1, m3, g, conv_w[0], conv_b, gla_norm_g, wc, wg, wo, tm=256)
    out = _ffn(x2, m3, g, f2_in, f2_out, mod0=6, gidx=4, tm=512, tf=512, rows_per_mod=SEQ)
    return out.reshape(BATCH, SEQ, D_MODEL)
```

```python
import functools

import jax
import jax.numpy as jnp
from jax import lax
from jax.experimental import pallas as pl
from jax.experimental.pallas import tpu as pltpu

F32 = jnp.float32
BF16 = jnp.bfloat16

D_MODEL = 2048
BATCH = 4
SEQ = 4096
GRID_W = 64
CTX_LEN = 256
D_FF = 5632
MACARON_WEIGHT = 0.5
CONV_W = 1024
GLA_HEADS = 4
GLA_DK = 1024
GLA_DV = 2048
HEAD_K = GLA_DK // GLA_HEADS
HEAD_V = GLA_DV // GLA_HEADS
GATE_RANK = 16
GATE_TAU = 16.0
N_MOD = 9
EPS = 1e-6
LOG2_E = 1.4426950408889634

LANES = 128
MOD_ROWS = 8
CTX_ROW = BATCH
GLA_CHUNK = 64
GLA_PREP_CHUNKS = 4
GLA_FINISH_ROWS = 256
LO_PAD = LANES

PROJ_TN = 1024
PROJ_HEAD_BLOCKS = 9
PROJ_BLOCKS = 13
PROJ_ROT = 8
PROJ_K_BLOCK = 4
COL_V, COL_R, COL_GA, COL_GB = 0, 2048, 4096, 6144
COL_BG, COL_CG, COL_HV, COL_Q, COL_K = 8192, 9216, 10240, 11264, 12288
P_COLS = PROJ_BLOCKS * PROJ_TN

VMEM_LIMIT = 56 * 1024 * 1024
GLA_VMEM_LIMIT = 60 * 1024 * 1024


def _dot(a, b):
    return jnp.dot(a, b, preferred_element_type=F32)


def _dot_t_lhs(a, b):
    return lax.dot_general(a, b, (((0,), (0,)), ((), ())), preferred_element_type=F32)


def _dot_t_rhs(a, b):
    return lax.dot_general(a, b, (((1,), (1,)), ((), ())), preferred_element_type=F32)


def _rms(x):
    return x * lax.rsqrt(jnp.mean(x * x, axis=-1, keepdims=True) + EPS)


def _silu(x):
    return x * jax.nn.sigmoid(x)


def _mod_kernel(c_ref, w_ref, b_ref, o_ref):
    s = _silu(c_ref[...]).astype(BF16)
    o_ref[...] = _dot(s, w_ref[...].astype(BF16)) + b_ref[...]


def _modulation(cc, w_mod, b_mod):
    n = w_mod.shape[1]
    tn = 1024
    return pl.pallas_call(
        _mod_kernel,
        grid=(n // tn,),
        in_specs=[
            pl.BlockSpec((MOD_ROWS, D_MODEL), lambda j: (0, 0)),
            pl.BlockSpec((D_MODEL, tn), lambda j: (0, j)),
            pl.BlockSpec((1, tn), lambda j: (0, j)),
        ],
        out_specs=pl.BlockSpec((MOD_ROWS, tn), lambda j: (0, j)),
        out_shape=jax.ShapeDtypeStruct((MOD_ROWS, n), F32),
        compiler_params=pltpu.CompilerParams(
            dimension_semantics=("arbitrary",), vmem_limit_bytes=VMEM_LIMIT),
        name="modulation",
    )(cc, w_mod, b_mod)


def _ffn_up_kernel(h_ref, m_ref, g_ref, wa_ref, wb_ref, act_ref, u_scr, *, mod0, gidx):
    @pl.when(pl.program_id(1) == 0)
    def _():
        shift = m_ref[0, mod0:mod0 + 1, :]
        scale = m_ref[0, mod0 + 1:mod0 + 2, :]
        u = _rms(h_ref[...]) * g_ref[gidx:gidx + 1, :]
        u_scr[...] = (u * (1.0 + scale) + shift).astype(BF16)

    u = u_scr[...]
    a = _dot(u, wa_ref[...])
    b = _dot(u, wb_ref[...])
    act_ref[...] = (_silu(b) * a).astype(BF16)


def _ffn_down_kernel(act_ref, h_ref, m_ref, g_ref, wo_ref, o_ref, *, mod0, gidx):
    y = _dot(act_ref[...], wo_ref[...])
    gate = m_ref[0, mod0 + 2:mod0 + 3, :]
    o_ref[...] = h_ref[...] + MACARON_WEIGHT * gate * (_rms(y) * g_ref[gidx + 1:gidx + 2, :])


def _ffn(h, m3, norm_g, w_in, w_out, *, mod0, gidx, tm_up, tf, tm_down, rows_per_mod, fixed_row=None):
    t = h.shape[0]
    nf = D_FF // tf

    def m_map(tm):
        if fixed_row is not None:
            return lambda i, *_: (fixed_row, 0, 0)
        tiles_per_mod = rows_per_mod // tm
        return lambda i, *_: (i // tiles_per_mod, 0, 0)

    act = pl.pallas_call(
        functools.partial(_ffn_up_kernel, mod0=mod0, gidx=gidx),
        grid=(t // tm_up, nf),
        in_specs=[
            pl.BlockSpec((tm_up, D_MODEL), lambda i, j: (i, 0)),
            pl.BlockSpec((1, N_MOD, D_MODEL), m_map(tm_up)),
            pl.BlockSpec((6, D_MODEL), lambda i, j: (0, 0)),
            pl.BlockSpec((D_MODEL, tf), lambda i, j: (0, j)),
            pl.BlockSpec((D_MODEL, tf), lambda i, j: (0, j + nf)),
        ],
        out_specs=pl.BlockSpec((tm_up, tf), lambda i, j: (i, j)),
        out_shape=jax.ShapeDtypeStruct((t, D_FF), BF16),
        scratch_shapes=[pltpu.VMEM((tm_up, D_MODEL), BF16)],
        compiler_params=pltpu.CompilerParams(
            dimension_semantics=("parallel", "arbitrary"), vmem_limit_bytes=VMEM_LIMIT),
        name="ffn_up",
    )(h, m3, norm_g, w_in, w_in)
    return pl.pallas_call(
        functools.partial(_ffn_down_kernel, mod0=mod0, gidx=gidx),
        grid=(t // tm_down,),
        in_specs=[
            pl.BlockSpec((tm_down, D_FF), lambda i: (i, 0)),
            pl.BlockSpec((tm_down, D_MODEL), lambda i: (i, 0)),
            pl.BlockSpec((1, N_MOD, D_MODEL), m_map(tm_down)),
            pl.BlockSpec((6, D_MODEL), lambda i: (0, 0)),
            pl.BlockSpec((D_FF, D_MODEL), lambda i: (0, 0)),
        ],
        out_specs=pl.BlockSpec((tm_down, D_MODEL), lambda i: (i, 0)),
        out_shape=jax.ShapeDtypeStruct((t, D_MODEL), F32),
        compiler_params=pltpu.CompilerParams(
            dimension_semantics=("parallel",), vmem_limit_bytes=VMEM_LIMIT),
        name="ffn_down",
    )(act, h, m3, norm_g, w_out)


def _proj_kernel(x_ref, m_ref, g_ref, w_ref, wg_ref, wlo_ref, p_ref, lo_ref, u_scr, *, head_blocks):
    n = pl.program_id(1)

    @pl.when(n == 0)
    def _():
        shift = m_ref[0, 3:4, :]
        scale = m_ref[0, 4:5, :]
        u = _rms(x_ref[...]) * g_ref[2:3, :]
        ub = (u * (1.0 + scale) + shift).astype(BF16)
        u_scr[...] = ub
        lo_ref[...] = _dot(ub, wlo_ref[...]).astype(BF16)

    @pl.when(n < head_blocks)
    def _():
        p_ref[...] = _dot(u_scr[...], w_ref[...]).astype(BF16)

    @pl.when(n >= head_blocks)
    def _():
        p_ref[...] = _dot(u_scr[...], wg_ref[...]).astype(BF16)


def _proj(x, m3, norm_g, w, w_gates, w_lo, *, tm, first_block, head_blocks, n_blocks, rot,
          rows_per_mod, fixed_row=None):
    t = x.shape[0]
    tn = PROJ_TN
    tiles_per_mod = rows_per_mod // tm
    if fixed_row is None:
        m_map = lambda i, n: (i // tiles_per_mod, 0, 0)
    else:
        m_map = lambda i, n: (fixed_row, 0, 0)
    return pl.pallas_call(
        functools.partial(_proj_kernel, head_blocks=head_blocks),
        grid=(t // tm, n_blocks),
        in_specs=[
            pl.BlockSpec((tm, D_MODEL), lambda i, n: (i, 0)),
            pl.BlockSpec((1, N_MOD, D_MODEL), m_map),
            pl.BlockSpec((6, D_MODEL), lambda i, n: (0, 0)),
            pl.BlockSpec((D_MODEL, tn), lambda i, n: (0, first_block + jnp.minimum(n, head_blocks - 1))),
            pl.BlockSpec((D_MODEL, tn), lambda i, n: (0, jnp.maximum(n - head_blocks, 0))),
            pl.BlockSpec((D_MODEL, LO_PAD), lambda i, n: (0, 0)),
        ],
        out_specs=[
            pl.BlockSpec((tm, tn), lambda i, n: (i, (n + rot) % n_blocks)),
            pl.BlockSpec((tm, LO_PAD), lambda i, n: (i, 0)),
        ],
        out_shape=[
            jax.ShapeDtypeStruct((t, n_blocks * tn), BF16),
            jax.ShapeDtypeStruct((t, LO_PAD), BF16),
        ],
        scratch_shapes=[pltpu.VMEM((tm, D_MODEL), BF16)],
        compiler_params=pltpu.CompilerParams(
            dimension_semantics=("parallel", "arbitrary"), vmem_limit_bytes=VMEM_LIMIT),
        name="proj",
    )(x, m3, norm_g, w, w_gates, w_lo)


def _log2_decay(lo_bf16, gu, bias):
    z = _dot(lo_bf16, gu) + bias
    return (jnp.minimum(z, 0.0) - jnp.log(1.0 + jnp.exp(-jnp.abs(z)))) * (LOG2_E / GATE_TAU)


def _split2(x):
    hi = x.astype(BF16)
    lo = (x - hi.astype(F32)).astype(BF16)
    return jnp.concatenate([hi, lo], axis=0)


def _tri2(n, upper, strict):
    r = lax.broadcasted_iota(jnp.int32, (n, 2 * n), 0)
    c = lax.broadcasted_iota(jnp.int32, (n, 2 * n), 1) & (n - 1)
    if upper:
        m = (c > r) if strict else (c >= r)
    else:
        m = (c < r) if strict else (c <= r)
    return m.astype(BF16)


def _gla_kernel(q_ref, k_ref, v_ref, lo_ref, ck_ref, cv_ref, clo_ref, gu_ref, gb_ref, on_ref,
                qd_f, kd_f, qd_b, kd_b, o_acc, o_bwd, a_f, a_b, st_f, st_b):
    C = GLA_CHUNK
    n_chunks = SEQ // C
    gu = gu_ref[0]
    bias = gb_ref[0]

    la_c = _log2_decay(clo_ref[0], gu, bias)
    ck = ck_ref[0].astype(F32)
    cv = cv_ref[0]
    dec_f = jnp.exp2(_dot(_tri2(CTX_LEN, True, True), _split2(la_c[:, :HEAD_K])))
    dec_b = jnp.exp2(_dot(_tri2(CTX_LEN, False, True), _split2(la_c[:, HEAD_K:])))
    st_f[...] = _dot_t_lhs(cv, (ck * dec_f).astype(BF16))
    st_b[...] = _dot_t_lhs(cv, (ck * dec_b).astype(BF16))

    tr = lax.broadcasted_iota(jnp.int32, (2 * C, 2 * C), 0)
    tc = lax.broadcasted_iota(jnp.int32, (2 * C, 2 * C), 1) & (C - 1)
    tri_fb = (((tr < C) & (tc <= tr)) | ((tr >= C) & (tc >= tr - C))).astype(BF16)
    pr = lax.broadcasted_iota(jnp.int32, (C, 2 * C), 0)
    pc = lax.broadcasted_iota(jnp.int32, (C, 2 * C), 1)
    keep_f = pc <= pr
    keep_b = pc - C >= pr

    def prepare(i, carry):
        r0 = pl.multiple_of(i * (GLA_PREP_CHUNKS * C), GLA_PREP_CHUNKS * C)
        la = _log2_decay(lo_ref[0, pl.ds(r0, GLA_PREP_CHUNKS * C), :], gu, bias)
        cum = [_dot(tri_fb, _split2(la[j * C:(j + 1) * C, :])) for j in range(GLA_PREP_CHUNKS)]
        scores = []
        for j in range(GLA_PREP_CHUNKS):
            rows = pl.ds(r0 + j * C, C)
            qc = q_ref[0, rows, :].astype(F32) * (HEAD_K ** -0.5)
            kc = k_ref[0, rows, :].astype(F32)
            b_f = cum[j][:C, :HEAD_K]
            b_b = cum[j][C:, HEAD_K:]
            parts = []
            for fwd, b in ((True, b_f), (False, b_b)):
                if fwd:
                    b_mid, b_last = b[C // 2 - 1:C // 2, :], b[C - 1:C, :]
                else:
                    b_mid, b_last = b[C // 2:C // 2 + 1, :], b[0:1, :]
                qs = qc * jnp.exp2(b - b_mid)
                ks = kc * jnp.exp2(b_mid - b)
                (qd_f if fwd else qd_b)[rows, :] = (qs * jnp.exp2(b_mid)).astype(BF16)
                (kd_f if fwd else kd_b)[rows, :] = (ks * jnp.exp2(b_last - b_mid)).astype(BF16)
                (a_f if fwd else a_b)[pl.ds(i * GLA_PREP_CHUNKS + j, 1), :] = jnp.exp2(b_last)
                parts.append((qs.astype(BF16), ks.astype(BF16)))
            scores.append(_dot_t_rhs(jnp.concatenate([parts[0][0], parts[1][0]], axis=0),
                                     jnp.concatenate([parts[0][1], parts[1][1]], axis=0)))
        for j in range(GLA_PREP_CHUNKS):
            rows = pl.ds(r0 + j * C, C)
            vc = v_ref[0, rows, :]
            sc = jnp.where(keep_f, scores[j][:C, :], jnp.where(keep_b, scores[j][C:, :], 0.0))
            o_acc[rows, :] = _dot(sc.astype(BF16), jnp.concatenate([vc, vc], axis=0))
        return carry

    def scan(i, carry):
        steps = ((i, qd_f, kd_f, a_f, st_f), (n_chunks - 1 - i, qd_b, kd_b, a_b, st_b))
        rows = [pl.ds(pl.multiple_of(c * C, C), C) for c, *_ in steps]
        incs = [_dot_t_lhs(v_ref[0, r, :], kd[r, :]) for r, (_, _, kd, _, _) in zip(rows, steps)]
        outs = [_dot_t_rhs(qd[r, :], st[...].astype(BF16)) for r, (_, qd, _, _, st) in zip(rows, steps)]
        o_acc[rows[0], :] += outs[0]
        o_bwd[rows[1], :] = outs[1]
        for inc, (c, _, _, a, st) in zip(incs, steps):
            st[...] = st[...] * a[pl.ds(c, 1), :] + inc
        return carry

    def finish(i, carry):
        rows = pl.ds(pl.multiple_of(i * GLA_FINISH_ROWS, GLA_FINISH_ROWS), GLA_FINISH_ROWS)
        on_ref[0, rows, :] = _rms(o_acc[rows, :] + o_bwd[rows, :]).astype(BF16)
        return carry

    lax.fori_loop(0, n_chunks // GLA_PREP_CHUNKS, prepare, 0)
    lax.fori_loop(0, n_chunks, scan, 0, unroll=4)
    lax.fori_loop(0, SEQ // GLA_FINISH_ROWS, finish, 0)


def _gla(p3, lo3, cp3, clo3, gu, gbias):
    qb, kb, vb = COL_Q // HEAD_K, COL_K // HEAD_K, COL_V // HEAD_V
    n_chunks = SEQ // GLA_CHUNK
    return pl.pallas_call(
        _gla_kernel,
        grid=(BATCH, GLA_HEADS),
        in_specs=[
            pl.BlockSpec((1, SEQ, HEAD_K), lambda b, h: (b, 0, qb + h)),
            pl.BlockSpec((1, SEQ, HEAD_K), lambda b, h: (b, 0, kb + h)),
            pl.BlockSpec((1, SEQ, HEAD_V), lambda b, h: (b, 0, vb + h)),
            pl.BlockSpec((1, SEQ, LO_PAD), lambda b, h: (b, 0, 0)),
            pl.BlockSpec((1, CTX_LEN, HEAD_K), lambda b, h: (b, 0, h)),
            pl.BlockSpec((1, CTX_LEN, HEAD_V), lambda b, h: (b, 0, GLA_DK // HEAD_V + h)),
            pl.BlockSpec((1, CTX_LEN, LO_PAD), lambda b, h: (b, 0, 0)),
            pl.BlockSpec((1, LO_PAD, 2 * HEAD_K), lambda b, h: (h, 0, 0)),
            pl.BlockSpec((1, 1, 2 * HEAD_K), lambda b, h: (h, 0, 0)),
        ],
        out_specs=pl.BlockSpec((1, SEQ, HEAD_V), lambda b, h: (b, 0, h)),
        out_shape=jax.ShapeDtypeStruct((BATCH, SEQ, GLA_DV), BF16),
        scratch_shapes=[
            pltpu.VMEM((SEQ, HEAD_K), BF16),
            pltpu.VMEM((SEQ, HEAD_K), BF16),
            pltpu.VMEM((SEQ, HEAD_K), BF16),
            pltpu.VMEM((SEQ, HEAD_K), BF16),
            pltpu.VMEM((SEQ, HEAD_V), F32),
            pltpu.VMEM((SEQ, HEAD_V), F32),
            pltpu.VMEM((n_chunks, HEAD_K), F32),
            pltpu.VMEM((n_chunks, HEAD_K), F32),
            pltpu.VMEM((HEAD_V, HEAD_K), F32),
            pltpu.VMEM((HEAD_V, HEAD_K), F32),
        ],
        compiler_params=pltpu.CompilerParams(
            dimension_semantics=("parallel", "arbitrary"), vmem_limit_bytes=GLA_VMEM_LIMIT),
        name="gla",
    )(p3, p3, p3, lo3, cp3, cp3, clo3, gu, gbias)


def _merge_kernel(bg_ref, cg_ref, hv_ref, r_ref, ga_ref, gb_ref, on_ref, x_ref, m_ref, g_ref,
                  cw_ref, cb_ref, gng_ref, wc_ref, wg_ref, wo_ref, o_ref):
    tm = x_ref.shape[0]
    z = cg_ref[...].astype(F32) * hv_ref[...].astype(F32)
    col = lax.broadcasted_iota(jnp.int32, (tm, 1), 0) & (GRID_W - 1)
    z_prev = jnp.where(col == 0, 0.0, pltpu.roll(z, 1, 0))
    z_next = jnp.where(col == GRID_W - 1, 0.0, pltpu.roll(z, tm - 1, 0))
    y = z_prev * cw_ref[0:1, :] + z * cw_ref[1:2, :] + z_next * cw_ref[2:3, :] + cb_ref[...]
    y_a = _dot((bg_ref[...].astype(F32) * y).astype(BF16), wc_ref[...])
    rd = on_ref[...].astype(F32) * gng_ref[...] * _silu(r_ref[...].astype(F32))
    y_b = _dot(rd.astype(BF16), wg_ref[...])
    mix = (jax.nn.sigmoid(ga_ref[...].astype(F32)) * y_a
           + jax.nn.sigmoid(gb_ref[...].astype(F32)) * y_b)
    yo = _dot(mix.astype(BF16), wo_ref[...])
    gate = m_ref[0, 5:6, :]
    o_ref[...] = x_ref[...] + gate * (_rms(yo) * g_ref[3:4, :])


def _merge(p, on, x1, m3, norm_g, conv_w, conv_b, gng, wc, wg, wo, *, tm):
    t = x1.shape[0]
    tiles_per_mod = SEQ // tm
    tok = lambda blk: (lambda i: (i, blk))
    const2 = lambda i: (0, 0)
    return pl.pallas_call(
        _merge_kernel,
        grid=(t // tm,),
        in_specs=[
            pl.BlockSpec((tm, CONV_W), tok(COL_BG // CONV_W)),
            pl.BlockSpec((tm, CONV_W), tok(COL_CG // CONV_W)),
            pl.BlockSpec((tm, CONV_W), tok(COL_HV // CONV_W)),
            pl.BlockSpec((tm, GLA_DV), tok(COL_R // GLA_DV)),
            pl.BlockSpec((tm, D_MODEL), tok(COL_GA // D_MODEL)),
            pl.BlockSpec((tm, D_MODEL), tok(COL_GB // D_MODEL)),
            pl.BlockSpec((tm, GLA_DV), tok(0)),
            pl.BlockSpec((tm, D_MODEL), tok(0)),
            pl.BlockSpec((1, N_MOD, D_MODEL), lambda i: (i // tiles_per_mod, 0, 0)),
            pl.BlockSpec((6, D_MODEL), const2),
            pl.BlockSpec((3, CONV_W), const2),
            pl.BlockSpec((1, CONV_W), const2),
            pl.BlockSpec((1, GLA_DV), const2),
            pl.BlockSpec((CONV_W, D_MODEL), const2),
            pl.BlockSpec((GLA_DV, D_MODEL), const2),
            pl.BlockSpec((D_MODEL, D_MODEL), const2),
        ],
        out_specs=pl.BlockSpec((tm, D_MODEL), tok(0)),
        out_shape=jax.ShapeDtypeStruct((t, D_MODEL), F32),
        compiler_params=pltpu.CompilerParams(
            dimension_semantics=("parallel",), vmem_limit_bytes=VMEM_LIMIT),
        name="merge",
    )(p, p, p, p, p, p, on, x1, m3, norm_g, conv_w, conv_b, gng, wc, wg, wo)


def kernel(x, c, ctx, c_ctx, w_mod, b_mod, norm_g, ffn1_w_in, ffn1_w_out, w_in, conv_w, conv_b,
           conv_out, gate_up, gate_bias, gla_norm_g, gla_out, w_o, ffn2_w_in, ffn2_w_out):
    assert w_mod.shape[0] == 1, "single-layer block"
    t = BATCH * SEQ
    tc = BATCH * CTX_LEN

    lo0 = PROJ_HEAD_BLOCKS * PROJ_TN
    w = w_in[0, :, :lo0].astype(BF16)
    w_gates = w_in[0, :, lo0 + 2 * GATE_RANK:].astype(BF16)
    w_lo = jnp.pad(w_in[0, :, lo0:lo0 + 2 * GATE_RANK], ((0, 0), (0, LO_PAD - 2 * GATE_RANK))).astype(BF16)
    per_head = lambda a: a.reshape(GATE_RANK, GLA_HEADS, HEAD_K).transpose(1, 0, 2)
    gu = jnp.zeros((GLA_HEADS, LO_PAD, 2 * HEAD_K), F32)
    gu = gu.at[:, 0:GATE_RANK, :HEAD_K].set(per_head(gate_up[0, 0]))
    gu = gu.at[:, GATE_RANK:2 * GATE_RANK, HEAD_K:].set(per_head(gate_up[0, 1]))
    gu = gu.astype(BF16)
    gbias = jnp.concatenate([gate_bias[0, 0].reshape(GLA_HEADS, 1, HEAD_K),
                             gate_bias[0, 1].reshape(GLA_HEADS, 1, HEAD_K)], axis=2)
    f1_in, f1_out = ffn1_w_in[0].astype(BF16), ffn1_w_out[0].astype(BF16)
    f2_in, f2_out = ffn2_w_in[0].astype(BF16), ffn2_w_out[0].astype(BF16)
    wc, wg, wo = conv_out[0].astype(BF16), gla_out[0].astype(BF16), w_o[0].astype(BF16)
    g = norm_g[0]

    cc = jnp.concatenate([c, c_ctx[None, :], jnp.zeros((MOD_ROWS - BATCH - 1, D_MODEL), F32)], axis=0)
    m3 = _modulation(cc, w_mod[0], b_mod).reshape(MOD_ROWS, N_MOD, D_MODEL)

    xf = x.reshape(t, D_MODEL)
    cf = ctx.reshape(tc, D_MODEL)
    x1 = _ffn(xf, m3, g, f1_in, f1_out, mod0=0, gidx=0, tm_up=1024, tf=512, tm_down=512, rows_per_mod=SEQ)
    c1 = _ffn(cf, m3, g, f1_in, f1_out, mod0=0, gidx=0, tm_up=1024, tf=512, tm_down=512,
              rows_per_mod=CTX_LEN, fixed_row=CTX_ROW)

    p, lo = _proj(x1, m3, g, w, w_gates, w_lo, tm=1024, first_block=0, head_blocks=PROJ_HEAD_BLOCKS,
                  n_blocks=PROJ_BLOCKS, rot=PROJ_ROT, rows_per_mod=SEQ)
    cp, clo = _proj(c1, m3, g, w, w_gates, w_lo, tm=1024, first_block=PROJ_K_BLOCK, head_blocks=3,
                    n_blocks=3, rot=0, rows_per_mod=CTX_LEN, fixed_row=CTX_ROW)

    on = _gla(p.reshape(BATCH, SEQ, P_COLS), lo.reshape(BATCH, SEQ, LO_PAD),
              cp.reshape(BATCH, CTX_LEN, GLA_DK + GLA_DV), clo.reshape(BATCH, CTX_LEN, LO_PAD),
              gu, gbias)

    x2 = _merge(p, on.reshape(t, GLA_DV), x1, m3, g, conv_w[0], conv_b, gla_norm_g, wc, wg, wo, tm=256)
    out = _ffn(x2, m3, g, f2_in, f2_out, mod0=6, gidx=4, tm_up=1024, tf=512, tm_down=512, rows_per_mod=SEQ)
    return out.reshape(BATCH, SEQ, D_MODEL)
```

```python
import functools

import jax
import jax.numpy as jnp
from jax import lax
from jax.experimental import pallas as pl
from jax.experimental.pallas import tpu as pltpu

F32 = jnp.float32
BF16 = jnp.bfloat16

D_MODEL = 2048
BATCH = 4
SEQ = 4096
GRID_W = 64
CTX_LEN = 256
D_FF = 5632
MACARON_WEIGHT = 0.5
CONV_W = 1024
GLA_HEADS = 4
GLA_DK = 1024
GLA_DV = 2048
HEAD_K = GLA_DK // GLA_HEADS
HEAD_V = GLA_DV // GLA_HEADS
GATE_RANK = 16
GATE_TAU = 16.0
N_MOD = 9
EPS = 1e-6
LOG2_E = 1.4426950408889634

LANES = 128
MOD_ROWS = 8
CTX_ROW = BATCH
GLA_CHUNK = 64
GLA_PREP_CHUNKS = 4
GLA_FINISH_ROWS = 256
FFN_UP_ROW_CHUNK = 256
LO_PAD = LANES

PROJ_TN = 1024
PROJ_HEAD_BLOCKS = 9
PROJ_BLOCKS = 13
PROJ_ROT = 8
PROJ_K_BLOCK = 4
COL_V, COL_R, COL_GA, COL_GB = 0, 2048, 4096, 6144
COL_BG, COL_CG, COL_HV, COL_Q, COL_K = 8192, 9216, 10240, 11264, 12288
P_COLS = PROJ_BLOCKS * PROJ_TN

VMEM_LIMIT = 56 * 1024 * 1024
GLA_VMEM_LIMIT = 60 * 1024 * 1024


def _dot(a, b):
    return jnp.dot(a, b, preferred_element_type=F32)


def _dot_t_lhs(a, b):
    return lax.dot_general(a, b, (((0,), (0,)), ((), ())), preferred_element_type=F32)


def _dot_t_rhs(a, b):
    return lax.dot_general(a, b, (((1,), (1,)), ((), ())), preferred_element_type=F32)


def _rms(x):
    return x * lax.rsqrt(jnp.mean(x * x, axis=-1, keepdims=True) + EPS)


def _silu(x):
    return x * jax.nn.sigmoid(x)


def _mod_kernel(c_ref, w_ref, b_ref, o_ref):
    s = _silu(c_ref[...]).astype(BF16)
    o_ref[...] = _dot(s, w_ref[...].astype(BF16)) + b_ref[...]


def _modulation(cc, w_mod, b_mod):
    n = w_mod.shape[1]
    tn = 1024
    return pl.pallas_call(
        _mod_kernel,
        grid=(n // tn,),
        in_specs=[
            pl.BlockSpec((MOD_ROWS, D_MODEL), lambda j: (0, 0)),
            pl.BlockSpec((D_MODEL, tn), lambda j: (0, j)),
            pl.BlockSpec((1, tn), lambda j: (0, j)),
        ],
        out_specs=pl.BlockSpec((MOD_ROWS, tn), lambda j: (0, j)),
        out_shape=jax.ShapeDtypeStruct((MOD_ROWS, n), F32),
        compiler_params=pltpu.CompilerParams(
            dimension_semantics=("arbitrary",), vmem_limit_bytes=VMEM_LIMIT),
        name="modulation",
    )(cc, w_mod, b_mod)


def _mod_norm(h, m_ref, g_ref, mod0, gidx):
    shift = m_ref[0, mod0:mod0 + 1, :]
    scale = m_ref[0, mod0 + 1:mod0 + 2, :]
    u = _rms(h) * g_ref[gidx:gidx + 1, :]
    return (u * (1.0 + scale) + shift).astype(BF16)


def _mod_row_map(tm, rows_per_mod, fixed_row):
    if fixed_row is not None:
        return lambda i, *_: (fixed_row, 0, 0)
    tiles_per_mod = rows_per_mod // tm
    return lambda i, *_: (i // tiles_per_mod, 0, 0)


def _prenorm_kernel(h_ref, m_ref, g_ref, u_ref, *, mod0, gidx):
    u_ref[...] = _mod_norm(h_ref[...], m_ref, g_ref, mod0, gidx)


def _prenorm(h, m3, norm_g, *, mod0, gidx, tm, rows_per_mod, fixed_row=None):
    t = h.shape[0]
    return pl.pallas_call(
        functools.partial(_prenorm_kernel, mod0=mod0, gidx=gidx),
        grid=(t // tm,),
        in_specs=[
            pl.BlockSpec((tm, D_MODEL), lambda i: (i, 0)),
            pl.BlockSpec((1, N_MOD, D_MODEL), _mod_row_map(tm, rows_per_mod, fixed_row)),
            pl.BlockSpec((6, D_MODEL), lambda i: (0, 0)),
        ],
        out_specs=pl.BlockSpec((tm, D_MODEL), lambda i: (i, 0)),
        out_shape=jax.ShapeDtypeStruct((t, D_MODEL), BF16),
        compiler_params=pltpu.CompilerParams(
            dimension_semantics=("parallel",), vmem_limit_bytes=VMEM_LIMIT),
        name="prenorm",
    )(h, m3, norm_g)


def _ffn_up_kernel(u_ref, wa_ref, wb_ref, act_ref):
    for r in range(0, u_ref.shape[0], FFN_UP_ROW_CHUNK):
        u = u_ref[r:r + FFN_UP_ROW_CHUNK, :]
        a = _dot(u, wa_ref[...])
        b = _dot(u, wb_ref[...])
        act_ref[r:r + FFN_UP_ROW_CHUNK, :] = (_silu(b) * a).astype(BF16)


def _ffn_up(u, w_in, *, tm, tf):
    t = u.shape[0]
    nf = D_FF // tf
    return pl.pallas_call(
        _ffn_up_kernel,
        grid=(nf, t // tm),
        in_specs=[
            pl.BlockSpec((tm, D_MODEL), lambda j, i: (i, 0)),
            pl.BlockSpec((D_MODEL, tf), lambda j, i: (0, j)),
            pl.BlockSpec((D_MODEL, tf), lambda j, i: (0, j + nf)),
        ],
        out_specs=pl.BlockSpec((tm, tf), lambda j, i: (i, j)),
        out_shape=jax.ShapeDtypeStruct((t, D_FF), BF16),
        compiler_params=pltpu.CompilerParams(
            dimension_semantics=("parallel", "parallel"), vmem_limit_bytes=VMEM_LIMIT),
        name="ffn_up",
    )(u, w_in, w_in)


def _ffn_down_kernel(*refs, mod0, gidx, next_mod0, next_gidx, row_chunk):
    if next_mod0 is None:
        act_ref, h_ref, m_ref, g_ref, wo_ref, o_ref = refs
    else:
        act_ref, h_ref, m_ref, g_ref, wo_ref, wlo_ref, o_ref, u_ref, lo_ref = refs
    gate = m_ref[0, mod0 + 2:mod0 + 3, :]
    us = []
    for r in range(0, h_ref.shape[0], row_chunk):
        rows = slice(r, r + row_chunk)
        y = _dot(act_ref[rows, :], wo_ref[...])
        o = h_ref[rows, :] + MACARON_WEIGHT * gate * (_rms(y) * g_ref[gidx + 1:gidx + 2, :])
        o_ref[rows, :] = o
        if next_mod0 is not None:
            us.append(_mod_norm(o, m_ref, g_ref, next_mod0, next_gidx))
            u_ref[rows, :] = us[-1]
    if next_mod0 is not None:
        lo_ref[...] = _dot(jnp.concatenate(us, axis=0), wlo_ref[...]).astype(BF16)


def _ffn_down(act, h, m3, norm_g, w_out, w_lo, *, mod0, gidx, next_mod0, next_gidx, tm, rows_per_mod,
              fixed_row=None):
    t = h.shape[0]
    emit_next = next_mod0 is not None
    tok = lambda i: (i, 0)
    const = lambda i: (0, 0)
    in_specs = [
        pl.BlockSpec((tm, D_FF), tok),
        pl.BlockSpec((tm, D_MODEL), tok),
        pl.BlockSpec((1, N_MOD, D_MODEL), _mod_row_map(tm, rows_per_mod, fixed_row)),
        pl.BlockSpec((6, D_MODEL), const),
        pl.BlockSpec((D_FF, D_MODEL), const),
    ]
    out_specs = [pl.BlockSpec((tm, D_MODEL), tok)]
    out_shape = [jax.ShapeDtypeStruct((t, D_MODEL), F32)]
    args = [act, h, m3, norm_g, w_out]
    if emit_next:
        in_specs.append(pl.BlockSpec((D_MODEL, LO_PAD), const))
        out_specs += [pl.BlockSpec((tm, D_MODEL), tok), pl.BlockSpec((tm, LO_PAD), tok)]
        out_shape += [jax.ShapeDtypeStruct((t, D_MODEL), BF16), jax.ShapeDtypeStruct((t, LO_PAD), BF16)]
        args.append(w_lo)
    return pl.pallas_call(
        functools.partial(_ffn_down_kernel, mod0=mod0, gidx=gidx, next_mod0=next_mod0, next_gidx=next_gidx,
                          row_chunk=tm // 2),
        grid=(t // tm,),
        in_specs=in_specs,
        out_specs=out_specs,
        out_shape=out_shape,
        compiler_params=pltpu.CompilerParams(
            dimension_semantics=("parallel",), vmem_limit_bytes=VMEM_LIMIT),
        name="ffn_down",
    )(*args)


def _proj_kernel(u_ref, w_ref, wg_ref, p_ref, *, head_blocks):
    n = pl.program_id(0)

    @pl.when(n < head_blocks)
    def _():
        p_ref[...] = _dot(u_ref[...], w_ref[...]).astype(BF16)

    @pl.when(n >= head_blocks)
    def _():
        p_ref[...] = _dot(u_ref[...], wg_ref[...]).astype(BF16)


def _proj(u, w, w_gates, *, tm, first_block, head_blocks, n_blocks, rot):
    t = u.shape[0]
    tn = PROJ_TN
    return pl.pallas_call(
        functools.partial(_proj_kernel, head_blocks=head_blocks),
        grid=(n_blocks, t // tm),
        in_specs=[
            pl.BlockSpec((tm, D_MODEL), lambda n, i: (i, 0)),
            pl.BlockSpec((D_MODEL, tn), lambda n, i: (0, first_block + jnp.minimum(n, head_blocks - 1))),
            pl.BlockSpec((D_MODEL, tn), lambda n, i: (0, jnp.maximum(n - head_blocks, 0))),
        ],
        out_specs=pl.BlockSpec((tm, tn), lambda n, i: (i, (n + rot) % n_blocks)),
        out_shape=jax.ShapeDtypeStruct((t, n_blocks * tn), BF16),
        compiler_params=pltpu.CompilerParams(
            dimension_semantics=("parallel", "parallel"), vmem_limit_bytes=VMEM_LIMIT),
        name="proj",
    )(u, w, w_gates)


def _log2_decay(lo_bf16, gu, bias):
    z = _dot(lo_bf16, gu) + bias
    return (jnp.minimum(z, 0.0) - jnp.log(1.0 + jnp.exp(-jnp.abs(z)))) * (LOG2_E / GATE_TAU)


def _split2(x):
    hi = x.astype(BF16)
    lo = (x - hi.astype(F32)).astype(BF16)
    return jnp.concatenate([hi, lo], axis=0)


def _tri2(n, upper, strict):
    r = lax.broadcasted_iota(jnp.int32, (n, 2 * n), 0)
    c = lax.broadcasted_iota(jnp.int32, (n, 2 * n), 1) & (n - 1)
    if upper:
        m = (c > r) if strict else (c >= r)
    else:
        m = (c < r) if strict else (c <= r)
    return m.astype(BF16)


def _gla_kernel(q_ref, k_ref, v_ref, lo_ref, ck_ref, cv_ref, clo_ref, gu_ref, gb_ref, on_ref,
                qd_f, kd_f, qd_b, kd_b, o_acc, o_bwd, a_f, a_b, st_f, st_b):
    C = GLA_CHUNK
    n_chunks = SEQ // C
    gu = gu_ref[0]
    bias = gb_ref[0]

    la_c = _log2_decay(clo_ref[0], gu, bias)
    ck = ck_ref[0].astype(F32)
    cv = cv_ref[0]
    dec_f = jnp.exp2(_dot(_tri2(CTX_LEN, True, True), _split2(la_c[:, :HEAD_K])))
    dec_b = jnp.exp2(_dot(_tri2(CTX_LEN, False, True), _split2(la_c[:, HEAD_K:])))
    st_f[...] = _dot_t_lhs(cv, (ck * dec_f).astype(BF16))
    st_b[...] = _dot_t_lhs(cv, (ck * dec_b).astype(BF16))

    tr = lax.broadcasted_iota(jnp.int32, (2 * C, 2 * C), 0)
    tc = lax.broadcasted_iota(jnp.int32, (2 * C, 2 * C), 1) & (C - 1)
    tri_fb = (((tr < C) & (tc <= tr)) | ((tr >= C) & (tc >= tr - C))).astype(BF16)
    pr = lax.broadcasted_iota(jnp.int32, (C, 2 * C), 0)
    pc = lax.broadcasted_iota(jnp.int32, (C, 2 * C), 1)
    keep_f = pc <= pr
    keep_b = pc - C >= pr

    def prepare(i, carry):
        r0 = pl.multiple_of(i * (GLA_PREP_CHUNKS * C), GLA_PREP_CHUNKS * C)
        la = _log2_decay(lo_ref[0, pl.ds(r0, GLA_PREP_CHUNKS * C), :], gu, bias)
        cum = [_dot(tri_fb, _split2(la[j * C:(j + 1) * C, :])) for j in range(GLA_PREP_CHUNKS)]
        scores = []
        for j in range(GLA_PREP_CHUNKS):
            rows = pl.ds(r0 + j * C, C)
            qc = q_ref[0, rows, :].astype(F32) * (HEAD_K ** -0.5)
            kc = k_ref[0, rows, :].astype(F32)
            b_f = cum[j][:C, :HEAD_K]
            b_b = cum[j][C:, HEAD_K:]
            parts = []
            for fwd, b in ((True, b_f), (False, b_b)):
                if fwd:
                    b_mid, b_last = b[C // 2 - 1:C // 2, :], b[C - 1:C, :]
                else:
                    b_mid, b_last = b[C // 2:C // 2 + 1, :], b[0:1, :]
                qs = qc * jnp.exp2(b - b_mid)
                ks = kc * jnp.exp2(b_mid - b)
                (qd_f if fwd else qd_b)[rows, :] = (qs * jnp.exp2(b_mid)).astype(BF16)
                (kd_f if fwd else kd_b)[rows, :] = (ks * jnp.exp2(b_last - b_mid)).astype(BF16)
                (a_f if fwd else a_b)[pl.ds(i * GLA_PREP_CHUNKS + j, 1), :] = jnp.exp2(b_last)
                parts.append((qs.astype(BF16), ks.astype(BF16)))
            scores.append(_dot_t_rhs(jnp.concatenate([parts[0][0], parts[1][0]], axis=0),
                                     jnp.concatenate([parts[0][1], parts[1][1]], axis=0)))
        for j in range(GLA_PREP_CHUNKS):
            rows = pl.ds(r0 + j * C, C)
            vc = v_ref[0, rows, :]
            sc = jnp.where(keep_f, scores[j][:C, :], jnp.where(keep_b, scores[j][C:, :], 0.0))
            o_acc[rows, :] = _dot(sc.astype(BF16), jnp.concatenate([vc, vc], axis=0))
        return carry

    def scan(i, carry):
        steps = ((i, qd_f, kd_f, a_f, st_f), (n_chunks - 1 - i, qd_b, kd_b, a_b, st_b))
        rows = [pl.ds(pl.multiple_of(c * C, C), C) for c, *_ in steps]
        incs = [_dot_t_lhs(v_ref[0, r, :], kd[r, :]) for r, (_, _, kd, _, _) in zip(rows, steps)]
        outs = [_dot_t_rhs(qd[r, :], st[...].astype(BF16)) for r, (_, qd, _, _, st) in zip(rows, steps)]
        o_acc[rows[0], :] += outs[0]
        o_bwd[rows[1], :] = outs[1]
        for inc, (c, _, _, a, st) in zip(incs, steps):
            st[...] = st[...] * a[pl.ds(c, 1), :] + inc
        return carry

    def finish(i, carry):
        rows = pl.ds(pl.multiple_of(i * GLA_FINISH_ROWS, GLA_FINISH_ROWS), GLA_FINISH_ROWS)
        on_ref[0, rows, :] = _rms(o_acc[rows, :] + o_bwd[rows, :]).astype(BF16)
        return carry

    lax.fori_loop(0, n_chunks // GLA_PREP_CHUNKS, prepare, 0)
    lax.fori_loop(0, n_chunks, scan, 0, unroll=4)
    lax.fori_loop(0, SEQ // GLA_FINISH_ROWS, finish, 0)


def _gla(p3, lo3, cp3, clo3, gu, gbias):
    qb, kb, vb = COL_Q // HEAD_K, COL_K // HEAD_K, COL_V // HEAD_V
    n_chunks = SEQ // GLA_CHUNK
    return pl.pallas_call(
        _gla_kernel,
        grid=(BATCH, GLA_HEADS),
        in_specs=[
            pl.BlockSpec((1, SEQ, HEAD_K), lambda b, h: (b, 0, qb + h)),
            pl.BlockSpec((1, SEQ, HEAD_K), lambda b, h: (b, 0, kb + h)),
            pl.BlockSpec((1, SEQ, HEAD_V), lambda b, h: (b, 0, vb + h)),
            pl.BlockSpec((1, SEQ, LO_PAD), lambda b, h: (b, 0, 0)),
            pl.BlockSpec((1, CTX_LEN, HEAD_K), lambda b, h: (b, 0, h)),
            pl.BlockSpec((1, CTX_LEN, HEAD_V), lambda b, h: (b, 0, GLA_DK // HEAD_V + h)),
            pl.BlockSpec((1, CTX_LEN, LO_PAD), lambda b, h: (b, 0, 0)),
            pl.BlockSpec((1, LO_PAD, 2 * HEAD_K), lambda b, h: (h, 0, 0)),
            pl.BlockSpec((1, 1, 2 * HEAD_K), lambda b, h: (h, 0, 0)),
        ],
        out_specs=pl.BlockSpec((1, SEQ, HEAD_V), lambda b, h: (b, 0, h)),
        out_shape=jax.ShapeDtypeStruct((BATCH, SEQ, GLA_DV), BF16),
        scratch_shapes=[
            pltpu.VMEM((SEQ, HEAD_K), BF16),
            pltpu.VMEM((SEQ, HEAD_K), BF16),
            pltpu.VMEM((SEQ, HEAD_K), BF16),
            pltpu.VMEM((SEQ, HEAD_K), BF16),
            pltpu.VMEM((SEQ, HEAD_V), F32),
            pltpu.VMEM((SEQ, HEAD_V), F32),
            pltpu.VMEM((n_chunks, HEAD_K), F32),
            pltpu.VMEM((n_chunks, HEAD_K), F32),
            pltpu.VMEM((HEAD_V, HEAD_K), F32),
            pltpu.VMEM((HEAD_V, HEAD_K), F32),
        ],
        compiler_params=pltpu.CompilerParams(
            dimension_semantics=("parallel", "arbitrary"), vmem_limit_bytes=GLA_VMEM_LIMIT),
        name="gla",
    )(p3, p3, p3, lo3, cp3, cp3, clo3, gu, gbias)


def _merge_kernel(bg_ref, cg_ref, hv_ref, r_ref, ga_ref, gb_ref, on_ref, x_ref, m_ref, g_ref,
                  cw_ref, cb_ref, gng_ref, wc_ref, wg_ref, wo_ref, o_ref, u_ref):
    tm = x_ref.shape[0]
    z = cg_ref[...].astype(F32) * hv_ref[...].astype(F32)
    col = lax.broadcasted_iota(jnp.int32, (tm, 1), 0) & (GRID_W - 1)
    z_prev = jnp.where(col == 0, 0.0, pltpu.roll(z, 1, 0))
    z_next = jnp.where(col == GRID_W - 1, 0.0, pltpu.roll(z, tm - 1, 0))
    y = z_prev * cw_ref[0:1, :] + z * cw_ref[1:2, :] + z_next * cw_ref[2:3, :] + cb_ref[...]
    y_a = _dot((bg_ref[...].astype(F32) * y).astype(BF16), wc_ref[...])
    rd = on_ref[...].astype(F32) * gng_ref[...] * _silu(r_ref[...].astype(F32))
    y_b = _dot(rd.astype(BF16), wg_ref[...])
    mix = (jax.nn.sigmoid(ga_ref[...].astype(F32)) * y_a
           + jax.nn.sigmoid(gb_ref[...].astype(F32)) * y_b)
    yo = _dot(mix.astype(BF16), wo_ref[...])
    gate = m_ref[0, 5:6, :]
    o = x_ref[...] + gate * (_rms(yo) * g_ref[3:4, :])
    o_ref[...] = o
    u_ref[...] = _mod_norm(o, m_ref, g_ref, 6, 4)


def _merge(p, on, x1, m3, norm_g, conv_w, conv_b, gng, wc, wg, wo, *, tm):
    t = x1.shape[0]
    tiles_per_mod = SEQ // tm
    tok = lambda blk: (lambda i: (i, blk))
    const2 = lambda i: (0, 0)
    return pl.pallas_call(
        _merge_kernel,
        grid=(t // tm,),
        in_specs=[
            pl.BlockSpec((tm, CONV_W), tok(COL_BG // CONV_W)),
            pl.BlockSpec((tm, CONV_W), tok(COL_CG // CONV_W)),
            pl.BlockSpec((tm, CONV_W), tok(COL_HV // CONV_W)),
            pl.BlockSpec((tm, GLA_DV), tok(COL_R // GLA_DV)),
            pl.BlockSpec((tm, D_MODEL), tok(COL_GA // D_MODEL)),
            pl.BlockSpec((tm, D_MODEL), tok(COL_GB // D_MODEL)),
            pl.BlockSpec((tm, GLA_DV), tok(0)),
            pl.BlockSpec((tm, D_MODEL), tok(0)),
            pl.BlockSpec((1, N_MOD, D_MODEL), lambda i: (i // tiles_per_mod, 0, 0)),
            pl.BlockSpec((6, D_MODEL), const2),
            pl.BlockSpec((3, CONV_W), const2),
            pl.BlockSpec((1, CONV_W), const2),
            pl.BlockSpec((1, GLA_DV), const2),
            pl.BlockSpec((CONV_W, D_MODEL), const2),
            pl.BlockSpec((GLA_DV, D_MODEL), const2),
            pl.BlockSpec((D_MODEL, D_MODEL), const2),
        ],
        out_specs=[pl.BlockSpec((tm, D_MODEL), tok(0)), pl.BlockSpec((tm, D_MODEL), tok(0))],
        out_shape=[jax.ShapeDtypeStruct((t, D_MODEL), F32), jax.ShapeDtypeStruct((t, D_MODEL), BF16)],
        compiler_params=pltpu.CompilerParams(
            dimension_semantics=("parallel",), vmem_limit_bytes=VMEM_LIMIT),
        name="merge",
    )(p, p, p, p, p, p, on, x1, m3, norm_g, conv_w, conv_b, gng, wc, wg, wo)


def kernel(x, c, ctx, c_ctx, w_mod, b_mod, norm_g, ffn1_w_in, ffn1_w_out, w_in, conv_w, conv_b,
           conv_out, gate_up, gate_bias, gla_norm_g, gla_out, w_o, ffn2_w_in, ffn2_w_out):
    assert w_mod.shape[0] == 1, "single-layer block"
    t = BATCH * SEQ
    tc = BATCH * CTX_LEN

    lo0 = PROJ_HEAD_BLOCKS * PROJ_TN
    w = w_in[0].astype(BF16)
    w_gates = w[:, lo0 + 2 * GATE_RANK:]
    w_lo = jnp.pad(w[:, lo0:lo0 + 2 * GATE_RANK], ((0, 0), (0, LO_PAD - 2 * GATE_RANK)))
    per_head = lambda a: a.reshape(GATE_RANK, GLA_HEADS, HEAD_K).transpose(1, 0, 2)
    gu = jnp.zeros((GLA_HEADS, LO_PAD, 2 * HEAD_K), F32)
    gu = gu.at[:, 0:GATE_RANK, :HEAD_K].set(per_head(gate_up[0, 0]))
    gu = gu.at[:, GATE_RANK:2 * GATE_RANK, HEAD_K:].set(per_head(gate_up[0, 1]))
    gu = gu.astype(BF16)
    gbias = jnp.concatenate([gate_bias[0, 0].reshape(GLA_HEADS, 1, HEAD_K),
                             gate_bias[0, 1].reshape(GLA_HEADS, 1, HEAD_K)], axis=2)
    f1_in, f1_out = ffn1_w_in[0].astype(BF16), ffn1_w_out[0].astype(BF16)
    f2_in, f2_out = ffn2_w_in[0].astype(BF16), ffn2_w_out[0].astype(BF16)
    wc, wg, wo = conv_out[0].astype(BF16), gla_out[0].astype(BF16), w_o[0].astype(BF16)
    g = norm_g[0]

    cc = jnp.concatenate([c, c_ctx[None, :], jnp.zeros((MOD_ROWS - BATCH - 1, D_MODEL), F32)], axis=0)
    m3 = _modulation(cc, w_mod[0], b_mod).reshape(MOD_ROWS, N_MOD, D_MODEL)

    xf = x.reshape(t, D_MODEL)
    cf = ctx.reshape(tc, D_MODEL)

    u1 = _prenorm(xf, m3, g, mod0=0, gidx=0, tm=1024, rows_per_mod=SEQ)
    x1, u2, lo = _ffn_down(_ffn_up(u1, f1_in, tm=2048, tf=512), xf, m3, g, f1_out, w_lo, mod0=0, gidx=0,
                           next_mod0=3, next_gidx=2, tm=256, rows_per_mod=SEQ)
    cu1 = _prenorm(cf, m3, g, mod0=0, gidx=0, tm=512, rows_per_mod=CTX_LEN, fixed_row=CTX_ROW)
    _, cu2, clo = _ffn_down(_ffn_up(cu1, f1_in, tm=1024, tf=512), cf, m3, g, f1_out, w_lo, mod0=0, gidx=0,
                            next_mod0=3, next_gidx=2, tm=256, rows_per_mod=CTX_LEN, fixed_row=CTX_ROW)

    p = _proj(u2, w, w_gates, tm=2048, first_block=0, head_blocks=PROJ_HEAD_BLOCKS, n_blocks=PROJ_BLOCKS,
              rot=PROJ_ROT)
    cp = _proj(cu2, w, w_gates, tm=1024, first_block=PROJ_K_BLOCK, head_blocks=3, n_blocks=3, rot=0)

    on = _gla(p.reshape(BATCH, SEQ, P_COLS), lo.reshape(BATCH, SEQ, LO_PAD),
              cp.reshape(BATCH, CTX_LEN, GLA_DK + GLA_DV), clo.reshape(BATCH, CTX_LEN, LO_PAD),
              gu, gbias)

    x2, u3 = _merge(p, on.reshape(t, GLA_DV), x1, m3, g, conv_w[0], conv_b, gla_norm_g, wc, wg, wo, tm=256)
    out = _ffn_down(_ffn_up(u3, f2_in, tm=2048, tf=512), x2, m3, g, f2_out, None, mod0=6, gidx=4,
                    next_mod0=None, next_gidx=None, tm=512, rows_per_mod=SEQ)[0]
    return out.reshape(BATCH, SEQ, D_MODEL)
```

```python
import functools

import jax
import jax.numpy as jnp
from jax import lax
from jax.experimental import pallas as pl
from jax.experimental.pallas import tpu as pltpu

F32 = jnp.float32
BF16 = jnp.bfloat16

D_MODEL = 2048
BATCH = 4
SEQ = 4096
GRID_W = 64
CTX_LEN = 256
D_FF = 5632
MACARON_WEIGHT = 0.5
CONV_W = 1024
GLA_HEADS = 4
GLA_DK = 1024
GLA_DV = 2048
HEAD_K = GLA_DK // GLA_HEADS
HEAD_V = GLA_DV // GLA_HEADS
GATE_RANK = 16
GATE_TAU = 16.0
N_MOD = 9
EPS = 1e-6
LOG2_E = 1.4426950408889634

LANES = 128
MOD_ROWS = 8
CTX_ROW = BATCH
GLA_CHUNK = 64
GLA_PREP_CHUNKS = 4
FFN_UP_ROW_CHUNK = 256
LO_PAD = LANES

PROJ_TN = 1024
PROJ_HEAD_BLOCKS = 9
PROJ_BLOCKS = 13
PROJ_ROT = 8
PROJ_K_BLOCK = 4
COL_V, COL_R, COL_GA, COL_GB = 0, 2048, 4096, 6144
COL_BG, COL_CG, COL_HV, COL_Q, COL_K = 8192, 9216, 10240, 11264, 12288
P_COLS = PROJ_BLOCKS * PROJ_TN

VMEM_LIMIT = 56 * 1024 * 1024
GLA_VMEM_LIMIT = 60 * 1024 * 1024


def _dot(a, b):
    return jnp.dot(a, b, preferred_element_type=F32)


def _dot_t_lhs(a, b):
    return lax.dot_general(a, b, (((0,), (0,)), ((), ())), preferred_element_type=F32)


def _dot_t_rhs(a, b):
    return lax.dot_general(a, b, (((1,), (1,)), ((), ())), preferred_element_type=F32)


def _rms(x):
    return x * lax.rsqrt(jnp.mean(x * x, axis=-1, keepdims=True) + EPS)


def _silu(x):
    return x * jax.nn.sigmoid(x)


def _mod_kernel(c_ref, w_ref, b_ref, o_ref):
    s = _silu(c_ref[...]).astype(BF16)
    o_ref[...] = _dot(s, w_ref[...].astype(BF16)) + b_ref[...]


def _modulation(cc, w_mod, b_mod):
    n = w_mod.shape[1]
    tn = 1024
    return pl.pallas_call(
        _mod_kernel,
        grid=(n // tn,),
        in_specs=[
            pl.BlockSpec((MOD_ROWS, D_MODEL), lambda j: (0, 0)),
            pl.BlockSpec((D_MODEL, tn), lambda j: (0, j)),
            pl.BlockSpec((1, tn), lambda j: (0, j)),
        ],
        out_specs=pl.BlockSpec((MOD_ROWS, tn), lambda j: (0, j)),
        out_shape=jax.ShapeDtypeStruct((MOD_ROWS, n), F32),
        compiler_params=pltpu.CompilerParams(
            dimension_semantics=("arbitrary",), vmem_limit_bytes=VMEM_LIMIT),
        name="modulation",
    )(cc, w_mod, b_mod)


def _mod_norm(h, m_ref, g_ref, mod0, gidx):
    shift = m_ref[0, mod0:mod0 + 1, :]
    scale = m_ref[0, mod0 + 1:mod0 + 2, :]
    u = _rms(h) * g_ref[gidx:gidx + 1, :]
    return (u * (1.0 + scale) + shift).astype(BF16)


def _mod_row_map(tm, rows_per_mod, fixed_row):
    if fixed_row is not None:
        return lambda i, *_: (fixed_row, 0, 0)
    tiles_per_mod = rows_per_mod // tm
    return lambda i, *_: (i // tiles_per_mod, 0, 0)


def _prenorm_kernel(h_ref, m_ref, g_ref, u_ref, *, mod0, gidx):
    u_ref[...] = _mod_norm(h_ref[...], m_ref, g_ref, mod0, gidx)


def _prenorm(h, m3, norm_g, *, mod0, gidx, tm, rows_per_mod, fixed_row=None):
    t = h.shape[0]
    return pl.pallas_call(
        functools.partial(_prenorm_kernel, mod0=mod0, gidx=gidx),
        grid=(t // tm,),
        in_specs=[
            pl.BlockSpec((tm, D_MODEL), lambda i: (i, 0)),
            pl.BlockSpec((1, N_MOD, D_MODEL), _mod_row_map(tm, rows_per_mod, fixed_row)),
            pl.BlockSpec((6, D_MODEL), lambda i: (0, 0)),
        ],
        out_specs=pl.BlockSpec((tm, D_MODEL), lambda i: (i, 0)),
        out_shape=jax.ShapeDtypeStruct((t, D_MODEL), BF16),
        compiler_params=pltpu.CompilerParams(
            dimension_semantics=("parallel",), vmem_limit_bytes=VMEM_LIMIT),
        name="prenorm",
    )(h, m3, norm_g)


def _ffn_up_kernel(u_ref, wa_ref, wb_ref, act_ref, wa_bf, wb_bf):
    @pl.when(pl.program_id(1) == 0)
    def _():
        wa_bf[...] = wa_ref[0].astype(BF16)
        wb_bf[...] = wb_ref[0].astype(BF16)

    for r in range(0, u_ref.shape[0], FFN_UP_ROW_CHUNK):
        u = u_ref[r:r + FFN_UP_ROW_CHUNK, :]
        a = _dot(u, wa_bf[...])
        b = _dot(u, wb_bf[...])
        act_ref[r:r + FFN_UP_ROW_CHUNK, :] = (_silu(b) * a).astype(BF16)


def _ffn_up(u, w_in3, *, tm, tf):
    t = u.shape[0]
    nf = D_FF // tf
    return pl.pallas_call(
        _ffn_up_kernel,
        grid=(nf, t // tm),
        in_specs=[
            pl.BlockSpec((tm, D_MODEL), lambda j, i: (i, 0)),
            pl.BlockSpec((1, D_MODEL, tf), lambda j, i: (0, 0, j)),
            pl.BlockSpec((1, D_MODEL, tf), lambda j, i: (0, 0, j + nf)),
        ],
        out_specs=pl.BlockSpec((tm, tf), lambda j, i: (i, j)),
        out_shape=jax.ShapeDtypeStruct((t, D_FF), BF16),
        scratch_shapes=[pltpu.VMEM((D_MODEL, tf), BF16), pltpu.VMEM((D_MODEL, tf), BF16)],
        compiler_params=pltpu.CompilerParams(
            dimension_semantics=("parallel", "arbitrary"), vmem_limit_bytes=VMEM_LIMIT),
        name="ffn_up",
    )(u, w_in3, w_in3)


def _ffn_down_kernel(*refs, mod0, gidx, next_mod0, next_gidx, row_chunk):
    if next_mod0 is None:
        act_ref, h_ref, m_ref, g_ref, wo_ref, o_ref = refs
    else:
        act_ref, h_ref, m_ref, g_ref, wo_ref, wlo_ref, o_ref, u_ref, lo_ref = refs
    gate = m_ref[0, mod0 + 2:mod0 + 3, :]
    us = []
    for r in range(0, h_ref.shape[0], row_chunk):
        rows = slice(r, r + row_chunk)
        y = _dot(act_ref[rows, :], wo_ref[...])
        o = h_ref[rows, :] + MACARON_WEIGHT * gate * (_rms(y) * g_ref[gidx + 1:gidx + 2, :])
        o_ref[rows, :] = o
        if next_mod0 is not None:
            us.append(_mod_norm(o, m_ref, g_ref, next_mod0, next_gidx))
            u_ref[rows, :] = us[-1]
    if next_mod0 is not None:
        lo_ref[...] = _dot(jnp.concatenate(us, axis=0), wlo_ref[...]).astype(BF16)


def _ffn_down(act, h, m3, norm_g, w_out, w_lo, *, mod0, gidx, next_mod0, next_gidx, tm, rows_per_mod,
              fixed_row=None):
    t = h.shape[0]
    emit_next = next_mod0 is not None
    tok = lambda i: (i, 0)
    const = lambda i: (0, 0)
    in_specs = [
        pl.BlockSpec((tm, D_FF), tok),
        pl.BlockSpec((tm, D_MODEL), tok),
        pl.BlockSpec((1, N_MOD, D_MODEL), _mod_row_map(tm, rows_per_mod, fixed_row)),
        pl.BlockSpec((6, D_MODEL), const),
        pl.BlockSpec((D_FF, D_MODEL), const),
    ]
    out_specs = [pl.BlockSpec((tm, D_MODEL), tok)]
    out_shape = [jax.ShapeDtypeStruct((t, D_MODEL), F32)]
    args = [act, h, m3, norm_g, w_out]
    if emit_next:
        in_specs.append(pl.BlockSpec((D_MODEL, LO_PAD), const))
        out_specs += [pl.BlockSpec((tm, D_MODEL), tok), pl.BlockSpec((tm, LO_PAD), tok)]
        out_shape += [jax.ShapeDtypeStruct((t, D_MODEL), BF16), jax.ShapeDtypeStruct((t, LO_PAD), BF16)]
        args.append(w_lo)
    return pl.pallas_call(
        functools.partial(_ffn_down_kernel, mod0=mod0, gidx=gidx, next_mod0=next_mod0, next_gidx=next_gidx,
                          row_chunk=tm // 2),
        grid=(t // tm,),
        in_specs=in_specs,
        out_specs=out_specs,
        out_shape=out_shape,
        compiler_params=pltpu.CompilerParams(
            dimension_semantics=("parallel",), vmem_limit_bytes=VMEM_LIMIT),
        name="ffn_down",
    )(*args)


def _proj_kernel(u_ref, w_ref, wg_ref, p_ref, *, head_blocks):
    n = pl.program_id(0)

    @pl.when(n < head_blocks)
    def _():
        p_ref[...] = _dot(u_ref[...], w_ref[...]).astype(BF16)

    @pl.when(n >= head_blocks)
    def _():
        p_ref[...] = _dot(u_ref[...], wg_ref[...]).astype(BF16)


def _proj(u, w, w_gates, *, tm, first_block, head_blocks, n_blocks, rot):
    t = u.shape[0]
    tn = PROJ_TN
    return pl.pallas_call(
        functools.partial(_proj_kernel, head_blocks=head_blocks),
        grid=(n_blocks, t // tm),
        in_specs=[
            pl.BlockSpec((tm, D_MODEL), lambda n, i: (i, 0)),
            pl.BlockSpec((D_MODEL, tn), lambda n, i: (0, first_block + jnp.minimum(n, head_blocks - 1))),
            pl.BlockSpec((D_MODEL, tn), lambda n, i: (0, jnp.maximum(n - head_blocks, 0))),
        ],
        out_specs=pl.BlockSpec((tm, tn), lambda n, i: (i, (n + rot) % n_blocks)),
        out_shape=jax.ShapeDtypeStruct((t, n_blocks * tn), BF16),
        compiler_params=pltpu.CompilerParams(
            dimension_semantics=("parallel", "parallel"), vmem_limit_bytes=VMEM_LIMIT),
        name="proj",
    )(u, w, w_gates)


def _log2_decay(lo_bf16, gu, bias):
    z = _dot(lo_bf16, gu) + bias
    return (jnp.minimum(z, 0.0) - jnp.log(1.0 + jnp.exp(-jnp.abs(z)))) * (LOG2_E / GATE_TAU)


def _split2(x):
    hi = x.astype(BF16)
    lo = (x - hi.astype(F32)).astype(BF16)
    return jnp.concatenate([hi, lo], axis=0)


def _tri2(n, upper, strict):
    r = lax.broadcasted_iota(jnp.int32, (n, 2 * n), 0)
    c = lax.broadcasted_iota(jnp.int32, (n, 2 * n), 1) & (n - 1)
    if upper:
        m = (c > r) if strict else (c >= r)
    else:
        m = (c < r) if strict else (c <= r)
    return m.astype(BF16)


def _gla_kernel(q_ref, k_ref, v_ref, lo_ref, ck_ref, cv_ref, clo_ref, gu_ref, gb_ref, on_ref,
                qd_f, kd_f, qd_b, kd_b, o_acc, o_bwd, a_f, a_b, st_f, st_b):
    C = GLA_CHUNK
    n_chunks = SEQ // C
    n_pairs = n_chunks // 2
    gu = gu_ref[0]
    bias = gb_ref[0]

    la_c = _log2_decay(clo_ref[0], gu, bias)
    ck = ck_ref[0].astype(F32)
    cv = cv_ref[0]
    dec_f = jnp.exp2(_dot(_tri2(CTX_LEN, True, True), _split2(la_c[:, :HEAD_K])))
    dec_b = jnp.exp2(_dot(_tri2(CTX_LEN, False, True), _split2(la_c[:, HEAD_K:])))
    st_f[...] = _dot_t_lhs(cv, (ck * dec_f).astype(BF16))
    st_b[...] = _dot_t_lhs(cv, (ck * dec_b).astype(BF16))

    tr = lax.broadcasted_iota(jnp.int32, (2 * C, 2 * C), 0)
    tc = lax.broadcasted_iota(jnp.int32, (2 * C, 2 * C), 1) & (C - 1)
    tri_fb = (((tr < C) & (tc <= tr)) | ((tr >= C) & (tc >= tr - C))).astype(BF16)
    pr = lax.broadcasted_iota(jnp.int32, (C, 2 * C), 0)
    pc = lax.broadcasted_iota(jnp.int32, (C, 2 * C), 1)
    keep_f = pc <= pr
    keep_b = pc - C >= pr

    def prepare(i, carry):
        r0 = pl.multiple_of(i * (GLA_PREP_CHUNKS * C), GLA_PREP_CHUNKS * C)
        la = _log2_decay(lo_ref[0, pl.ds(r0, GLA_PREP_CHUNKS * C), :], gu, bias)
        cum = [_dot(tri_fb, _split2(la[j * C:(j + 1) * C, :])) for j in range(GLA_PREP_CHUNKS)]
        chunks = []
        for j in range(GLA_PREP_CHUNKS):
            rows = pl.ds(r0 + j * C, C)
            qc = q_ref[0, rows, :].astype(F32) * (HEAD_K ** -0.5)
            kc = k_ref[0, rows, :].astype(F32)
            b_f = cum[j][:C, :HEAD_K]
            b_b = cum[j][C:, HEAD_K:]
            dirs = []
            for fwd, b in ((True, b_f), (False, b_b)):
                if fwd:
                    b_mid, b_last = b[C // 2 - 1:C // 2, :], b[C - 1:C, :]
                else:
                    b_mid, b_last = b[C // 2:C // 2 + 1, :], b[0:1, :]
                qs = qc * jnp.exp2(b - b_mid)
                ks = kc * jnp.exp2(b_mid - b)
                dirs.append(dict(qs=qs.astype(BF16), ks=ks.astype(BF16), qd=qs * jnp.exp2(b_mid),
                                 kd=ks * jnp.exp2(b_last - b_mid), a=jnp.exp2(b_last)))
            scores = _dot_t_rhs(jnp.concatenate([dirs[0]["qs"], dirs[1]["qs"]], axis=0),
                                jnp.concatenate([dirs[0]["ks"], dirs[1]["ks"]], axis=0))
            chunks.append(dict(rows=rows, f=dirs[0], b=dirs[1], scores=scores))
        crosses = []
        for p in range(GLA_PREP_CHUNKS // 2):
            lo_c, hi_c = chunks[2 * p], chunks[2 * p + 1]
            lo_f, hi_f, lo_b, hi_b = lo_c["f"], hi_c["f"], lo_c["b"], hi_c["b"]
            qd_f[lo_c["rows"], :] = lo_f["qd"].astype(BF16)
            qd_f[hi_c["rows"], :] = (hi_f["qd"] * lo_f["a"]).astype(BF16)
            kd_f[lo_c["rows"], :] = (lo_f["kd"] * hi_f["a"]).astype(BF16)
            kd_f[hi_c["rows"], :] = hi_f["kd"].astype(BF16)
            qd_b[hi_c["rows"], :] = hi_b["qd"].astype(BF16)
            qd_b[lo_c["rows"], :] = (lo_b["qd"] * hi_b["a"]).astype(BF16)
            kd_b[hi_c["rows"], :] = (hi_b["kd"] * lo_b["a"]).astype(BF16)
            kd_b[lo_c["rows"], :] = lo_b["kd"].astype(BF16)
            pair = pl.ds(i * (GLA_PREP_CHUNKS // 2) + p, 1)
            a_f[pair, :] = lo_f["a"] * hi_f["a"]
            a_b[pair, :] = lo_b["a"] * hi_b["a"]
            crosses.append(_dot_t_rhs(
                jnp.concatenate([hi_f["qd"].astype(BF16), lo_b["qd"].astype(BF16)], axis=0),
                jnp.concatenate([lo_f["kd"].astype(BF16), hi_b["kd"].astype(BF16)], axis=0)))
        for p in range(GLA_PREP_CHUNKS // 2):
            lo_c, hi_c = chunks[2 * p], chunks[2 * p + 1]
            v_lo = v_ref[0, lo_c["rows"], :]
            v_hi = v_ref[0, hi_c["rows"], :]
            zero = jnp.zeros((C, 2 * C), F32)
            w_lo = jnp.concatenate(
                [jnp.where(keep_f, lo_c["scores"][:C, :], jnp.where(keep_b, lo_c["scores"][C:, :], 0.0)),
                 jnp.where(pc >= C, crosses[p][C:, :], zero)], axis=1)
            o_acc[lo_c["rows"], :] = _dot(w_lo.astype(BF16), jnp.concatenate([v_lo, v_lo, v_lo, v_hi], axis=0))
            w_hi = jnp.concatenate(
                [jnp.where(keep_f, hi_c["scores"][:C, :], jnp.where(keep_b, hi_c["scores"][C:, :], 0.0)),
                 jnp.where(pc < C, crosses[p][:C, :], zero)], axis=1)
            o_acc[hi_c["rows"], :] = _dot(w_hi.astype(BF16), jnp.concatenate([v_hi, v_hi, v_lo, v_lo], axis=0))
        return carry

    def scan(i, carry, final):
        steps = ((i, qd_f, kd_f, a_f, st_f), (n_pairs - 1 - i, qd_b, kd_b, a_b, st_b))
        rows = [pl.ds(pl.multiple_of(c * (2 * C), 2 * C), 2 * C) for c, *_ in steps]
        incs = [_dot_t_lhs(v_ref[0, r, :], kd[r, :]) for r, (_, _, kd, _, _) in zip(rows, steps)]
        outs = [_dot_t_rhs(qd[r, :], st[...].astype(BF16)) for r, (_, qd, _, _, st) in zip(rows, steps)]
        if not final:
            o_acc[rows[0], :] += outs[0]
            upper = pl.ds(pl.multiple_of((n_pairs // 2 - 1 - i) * (2 * C), 2 * C), 2 * C)
            o_bwd[upper, :] = outs[1]
        else:
            upper = pl.ds(pl.multiple_of((i - n_pairs // 2) * (2 * C), 2 * C), 2 * C)
            on_ref[0, rows[0], :] = _rms(o_acc[rows[0], :] + o_bwd[upper, :] + outs[0]).astype(BF16)
            on_ref[0, rows[1], :] = _rms(o_acc[rows[1], :] + outs[1]).astype(BF16)
        for inc, (c, _, _, a, st) in zip(incs, steps):
            st[...] = st[...] * a[pl.ds(c, 1), :] + inc
        return carry

    lax.fori_loop(0, n_chunks // GLA_PREP_CHUNKS, prepare, 0)
    lax.fori_loop(0, n_pairs // 2, functools.partial(scan, final=False), 0, unroll=2)
    lax.fori_loop(n_pairs // 2, n_pairs, functools.partial(scan, final=True), 0, unroll=2)


def _gla(p3, lo3, cp3, clo3, gu, gbias):
    qb, kb, vb = COL_Q // HEAD_K, COL_K // HEAD_K, COL_V // HEAD_V
    n_chunks = SEQ // GLA_CHUNK
    return pl.pallas_call(
        _gla_kernel,
        grid=(BATCH, GLA_HEADS),
        in_specs=[
            pl.BlockSpec((1, SEQ, HEAD_K), lambda b, h: (b, 0, qb + h)),
            pl.BlockSpec((1, SEQ, HEAD_K), lambda b, h: (b, 0, kb + h)),
            pl.BlockSpec((1, SEQ, HEAD_V), lambda b, h: (b, 0, vb + h)),
            pl.BlockSpec((1, SEQ, LO_PAD), lambda b, h: (b, 0, 0)),
            pl.BlockSpec((1, CTX_LEN, HEAD_K), lambda b, h: (b, 0, h)),
            pl.BlockSpec((1, CTX_LEN, HEAD_V), lambda b, h: (b, 0, GLA_DK // HEAD_V + h)),
            pl.BlockSpec((1, CTX_LEN, LO_PAD), lambda b, h: (b, 0, 0)),
            pl.BlockSpec((1, LO_PAD, 2 * HEAD_K), lambda b, h: (h, 0, 0)),
            pl.BlockSpec((1, 1, 2 * HEAD_K), lambda b, h: (h, 0, 0)),
        ],
        out_specs=pl.BlockSpec((1, SEQ, HEAD_V), lambda b, h: (b, 0, h)),
        out_shape=jax.ShapeDtypeStruct((BATCH, SEQ, GLA_DV), BF16),
        scratch_shapes=[
            pltpu.VMEM((SEQ, HEAD_K), BF16),
            pltpu.VMEM((SEQ, HEAD_K), BF16),
            pltpu.VMEM((SEQ, HEAD_K), BF16),
            pltpu.VMEM((SEQ, HEAD_K), BF16),
            pltpu.VMEM((SEQ, HEAD_V), F32),
            pltpu.VMEM((SEQ // 2, HEAD_V), F32),
            pltpu.VMEM((n_chunks // 2, HEAD_K), F32),
            pltpu.VMEM((n_chunks // 2, HEAD_K), F32),
            pltpu.VMEM((HEAD_V, HEAD_K), F32),
            pltpu.VMEM((HEAD_V, HEAD_K), F32),
        ],
        compiler_params=pltpu.CompilerParams(
            dimension_semantics=("parallel", "arbitrary"), vmem_limit_bytes=GLA_VMEM_LIMIT),
        name="gla",
    )(p3, p3, p3, lo3, cp3, cp3, clo3, gu, gbias)


def _merge_kernel(bg_ref, cg_ref, hv_ref, r_ref, ga_ref, gb_ref, on_ref, x_ref, m_ref, g_ref,
                  cw_ref, cb_ref, gng_ref, wc_ref, wg_ref, wo_ref, o_ref, u_ref):
    tm = x_ref.shape[0]
    z = cg_ref[...].astype(F32) * hv_ref[...].astype(F32)
    col = lax.broadcasted_iota(jnp.int32, (tm, 1), 0) & (GRID_W - 1)
    z_prev = jnp.where(col == 0, 0.0, pltpu.roll(z, 1, 0))
    z_next = jnp.where(col == GRID_W - 1, 0.0, pltpu.roll(z, tm - 1, 0))
    y = z_prev * cw_ref[0:1, :] + z * cw_ref[1:2, :] + z_next * cw_ref[2:3, :] + cb_ref[...]
    y_a = _dot((bg_ref[...].astype(F32) * y).astype(BF16), wc_ref[...])
    rd = on_ref[...].astype(F32) * gng_ref[...] * _silu(r_ref[...].astype(F32))
    y_b = _dot(rd.astype(BF16), wg_ref[...])
    mix = (jax.nn.sigmoid(ga_ref[...].astype(F32)) * y_a
           + jax.nn.sigmoid(gb_ref[...].astype(F32)) * y_b)
    yo = _dot(mix.astype(BF16), wo_ref[...])
    gate = m_ref[0, 5:6, :]
    o = x_ref[...] + gate * (_rms(yo) * g_ref[3:4, :])
    o_ref[...] = o
    u_ref[...] = _mod_norm(o, m_ref, g_ref, 6, 4)


def _merge(p, on, x1, m3, norm_g, conv_w, conv_b, gng, wc, wg, wo, *, tm):
    t = x1.shape[0]
    tiles_per_mod = SEQ // tm
    tok = lambda blk: (lambda i: (i, blk))
    const2 = lambda i: (0, 0)
    return pl.pallas_call(
        _merge_kernel,
        grid=(t // tm,),
        in_specs=[
            pl.BlockSpec((tm, CONV_W), tok(COL_BG // CONV_W)),
            pl.BlockSpec((tm, CONV_W), tok(COL_CG // CONV_W)),
            pl.BlockSpec((tm, CONV_W), tok(COL_HV // CONV_W)),
            pl.BlockSpec((tm, GLA_DV), tok(COL_R // GLA_DV)),
            pl.BlockSpec((tm, D_MODEL), tok(COL_GA // D_MODEL)),
            pl.BlockSpec((tm, D_MODEL), tok(COL_GB // D_MODEL)),
            pl.BlockSpec((tm, GLA_DV), tok(0)),
            pl.BlockSpec((tm, D_MODEL), tok(0)),
            pl.BlockSpec((1, N_MOD, D_MODEL), lambda i: (i // tiles_per_mod, 0, 0)),
            pl.BlockSpec((6, D_MODEL), const2),
            pl.BlockSpec((3, CONV_W), const2),
            pl.BlockSpec((1, CONV_W), const2),
            pl.BlockSpec((1, GLA_DV), const2),
            pl.BlockSpec((CONV_W, D_MODEL), const2),
            pl.BlockSpec((GLA_DV, D_MODEL), const2),
            pl.BlockSpec((D_MODEL, D_MODEL), const2),
        ],
        out_specs=[pl.BlockSpec((tm, D_MODEL), tok(0)), pl.BlockSpec((tm, D_MODEL), tok(0))],
        out_shape=[jax.ShapeDtypeStruct((t, D_MODEL), F32), jax.ShapeDtypeStruct((t, D_MODEL), BF16)],
        compiler_params=pltpu.CompilerParams(
            dimension_semantics=("parallel",), vmem_limit_bytes=VMEM_LIMIT),
        name="merge",
    )(p, p, p, p, p, p, on, x1, m3, norm_g, conv_w, conv_b, gng, wc, wg, wo)


def kernel(x, c, ctx, c_ctx, w_mod, b_mod, norm_g, ffn1_w_in, ffn1_w_out, w_in, conv_w, conv_b,
           conv_out, gate_up, gate_bias, gla_norm_g, gla_out, w_o, ffn2_w_in, ffn2_w_out):
    assert w_mod.shape[0] == 1, "single-layer block"
    t = BATCH * SEQ
    tc = BATCH * CTX_LEN

    lo0 = PROJ_HEAD_BLOCKS * PROJ_TN
    w = w_in[0].astype(BF16)
    w_gates = w[:, lo0 + 2 * GATE_RANK:]
    w_lo = jnp.pad(w[:, lo0:lo0 + 2 * GATE_RANK], ((0, 0), (0, LO_PAD - 2 * GATE_RANK)))
    per_head = lambda a: a.reshape(GATE_RANK, GLA_HEADS, HEAD_K).transpose(1, 0, 2)
    gu = jnp.zeros((GLA_HEADS, LO_PAD, 2 * HEAD_K), F32)
    gu = gu.at[:, 0:GATE_RANK, :HEAD_K].set(per_head(gate_up[0, 0]))
    gu = gu.at[:, GATE_RANK:2 * GATE_RANK, HEAD_K:].set(per_head(gate_up[0, 1]))
    gu = gu.astype(BF16)
    gbias = jnp.concatenate([gate_bias[0, 0].reshape(GLA_HEADS, 1, HEAD_K),
                             gate_bias[0, 1].reshape(GLA_HEADS, 1, HEAD_K)], axis=2)
    f1_in, f1_out = ffn1_w_in, ffn1_w_out[0].astype(BF16)
    f2_in, f2_out = ffn2_w_in, ffn2_w_out[0].astype(BF16)
    wc, wg, wo = conv_out[0].astype(BF16), gla_out[0].astype(BF16), w_o[0].astype(BF16)
    g = norm_g[0]

    cc = jnp.concatenate([c, c_ctx[None, :], jnp.zeros((MOD_ROWS - BATCH - 1, D_MODEL), F32)], axis=0)
    m3 = _modulation(cc, w_mod[0], b_mod).reshape(MOD_ROWS, N_MOD, D_MODEL)

    xf = x.reshape(t, D_MODEL)
    cf = ctx.reshape(tc, D_MODEL)

    u1 = _prenorm(xf, m3, g, mod0=0, gidx=0, tm=1024, rows_per_mod=SEQ)
    x1, u2, lo = _ffn_down(_ffn_up(u1, f1_in, tm=2048, tf=512), xf, m3, g, f1_out, w_lo, mod0=0, gidx=0,
                           next_mod0=3, next_gidx=2, tm=256, rows_per_mod=SEQ)
    cu1 = _prenorm(cf, m3, g, mod0=0, gidx=0, tm=512, rows_per_mod=CTX_LEN, fixed_row=CTX_ROW)
    _, cu2, clo = _ffn_down(_ffn_up(cu1, f1_in, tm=1024, tf=512), cf, m3, g, f1_out, w_lo, mod0=0, gidx=0,
                            next_mod0=3, next_gidx=2, tm=256, rows_per_mod=CTX_LEN, fixed_row=CTX_ROW)

    p = _proj(u2, w, w_gates, tm=2048, first_block=0, head_blocks=PROJ_HEAD_BLOCKS, n_blocks=PROJ_BLOCKS,
              rot=PROJ_ROT)
    cp = _proj(cu2, w, w_gates, tm=1024, first_block=PROJ_K_BLOCK, head_blocks=3, n_blocks=3, rot=0)

    on = _gla(p.reshape(BATCH, SEQ, P_COLS), lo.reshape(BATCH, SEQ, LO_PAD),
              cp.reshape(BATCH, CTX_LEN, GLA_DK + GLA_DV), clo.reshape(BATCH, CTX_LEN, LO_PAD),
              gu, gbias)

    x2, u3 = _merge(p, on.reshape(t, GLA_DV), x1, m3, g, conv_w[0], conv_b, gla_norm_g, wc, wg, wo, tm=256)
    out = _ffn_down(_ffn_up(u3, f2_in, tm=2048, tf=512), x2, m3, g, f2_out, None, mod0=6, gidx=4,
                    next_mod0=None, next_gidx=None, tm=512, rows_per_mod=SEQ)[0]
    return out.reshape(BATCH, SEQ, D_MODEL)
```

```python
import functools

import jax
import jax.numpy as jnp
from jax import lax
from jax.experimental import pallas as pl
from jax.experimental.pallas import tpu as pltpu

F32 = jnp.float32
BF16 = jnp.bfloat16

D_MODEL = 2048
BATCH = 4
SEQ = 4096
GRID_W = 64
CTX_LEN = 256
D_FF = 5632
MACARON_WEIGHT = 0.5
CONV_W = 1024
GLA_HEADS = 4
GLA_DK = 1024
GLA_DV = 2048
HEAD_K = GLA_DK // GLA_HEADS
HEAD_V = GLA_DV // GLA_HEADS
GATE_RANK = 16
GATE_TAU = 16.0
N_MOD = 9
EPS = 1e-6
LOG2_E = 1.4426950408889634
Q_SCALE_LOG2 = 4.0
assert 2.0 ** (2 * Q_SCALE_LOG2) == HEAD_K

LANES = 128
MOD_ROWS = 8
CTX_ROW = BATCH
GLA_CHUNK = 64
GLA_PREP_CHUNKS = 8
FFN_UP_ROW_CHUNK = 256
LO_PAD = LANES

PROJ_TN = 1024
PROJ_HEAD_BLOCKS = 9
PROJ_BLOCKS = 13
PROJ_ROT = 8
PROJ_K_BLOCK = 4
COL_V, COL_R, COL_GA, COL_GB = 0, 2048, 4096, 6144
COL_BG, COL_CG, COL_HV, COL_Q, COL_K = 8192, 9216, 10240, 11264, 12288
P_COLS = PROJ_BLOCKS * PROJ_TN

VMEM_LIMIT = 56 * 1024 * 1024
BIG_VMEM_LIMIT = 60 * 1024 * 1024


def _dot(a, b):
    return jnp.dot(a, b, preferred_element_type=F32)


def _dot_t_lhs(a, b):
    return lax.dot_general(a, b, (((0,), (0,)), ((), ())), preferred_element_type=F32)


def _dot_t_rhs(a, b):
    return lax.dot_general(a, b, (((1,), (1,)), ((), ())), preferred_element_type=F32)


def _rms(x):
    return x * lax.rsqrt(jnp.mean(x * x, axis=-1, keepdims=True) + EPS)


def _silu(x):
    return x * jax.nn.sigmoid(x)


def _mod_kernel(c_ref, w_ref, b_ref, o_ref):
    s = _silu(c_ref[...]).astype(BF16)
    o_ref[...] = _dot(s, w_ref[...].astype(BF16)) + b_ref[...]


def _modulation(cc, w_mod, b_mod):
    n = w_mod.shape[1]
    tn = 1024
    return pl.pallas_call(
        _mod_kernel,
        grid=(n // tn,),
        in_specs=[
            pl.BlockSpec((MOD_ROWS, D_MODEL), lambda j: (0, 0)),
            pl.BlockSpec((D_MODEL, tn), lambda j: (0, j)),
            pl.BlockSpec((1, tn), lambda j: (0, j)),
        ],
        out_specs=pl.BlockSpec((MOD_ROWS, tn), lambda j: (0, j)),
        out_shape=jax.ShapeDtypeStruct((MOD_ROWS, n), F32),
        compiler_params=pltpu.CompilerParams(
            dimension_semantics=("arbitrary",), vmem_limit_bytes=VMEM_LIMIT),
        name="modulation",
    )(cc, w_mod, b_mod)


def _mod_norm(h, m_ref, g_ref, mod0, gidx):
    shift = m_ref[0, mod0:mod0 + 1, :]
    scale = m_ref[0, mod0 + 1:mod0 + 2, :]
    u = _rms(h) * g_ref[gidx:gidx + 1, :]
    return (u * (1.0 + scale) + shift).astype(BF16)


def _mod_row_map(tm, rows_per_mod, fixed_row):
    if fixed_row is not None:
        return lambda i, *_: (fixed_row, 0, 0)
    tiles_per_mod = rows_per_mod // tm
    return lambda i, *_: (i // tiles_per_mod, 0, 0)


def _prenorm_kernel(h_ref, m_ref, g_ref, u_ref, *, mod0, gidx):
    u_ref[...] = _mod_norm(h_ref[...], m_ref, g_ref, mod0, gidx)


def _prenorm(h, m3, norm_g, *, mod0, gidx, tm, rows_per_mod, fixed_row=None):
    t = h.shape[0]
    return pl.pallas_call(
        functools.partial(_prenorm_kernel, mod0=mod0, gidx=gidx),
        grid=(t // tm,),
        in_specs=[
            pl.BlockSpec((tm, D_MODEL), lambda i: (i, 0)),
            pl.BlockSpec((1, N_MOD, D_MODEL), _mod_row_map(tm, rows_per_mod, fixed_row)),
            pl.BlockSpec((6, D_MODEL), lambda i: (0, 0)),
        ],
        out_specs=pl.BlockSpec((tm, D_MODEL), lambda i: (i, 0)),
        out_shape=jax.ShapeDtypeStruct((t, D_MODEL), BF16),
        compiler_params=pltpu.CompilerParams(
            dimension_semantics=("parallel",), vmem_limit_bytes=VMEM_LIMIT),
        name="prenorm",
    )(h, m3, norm_g)


def _ffn_up_kernel(u_ref, wa_ref, wb_ref, act_ref, wa_bf, wb_bf):
    @pl.when(pl.program_id(1) == 0)
    def _():
        wa_bf[...] = wa_ref[0].astype(BF16)
        wb_bf[...] = wb_ref[0].astype(BF16)

    for r in range(0, u_ref.shape[0], FFN_UP_ROW_CHUNK):
        u = u_ref[r:r + FFN_UP_ROW_CHUNK, :]
        a = _dot(u, wa_bf[...])
        b = _dot(u, wb_bf[...])
        act_ref[r:r + FFN_UP_ROW_CHUNK, :] = (_silu(b) * a).astype(BF16)


def _ffn_up(u, w_in3, *, tm, tf):
    t = u.shape[0]
    nf = D_FF // tf
    return pl.pallas_call(
        _ffn_up_kernel,
        grid=(nf, t // tm),
        in_specs=[
            pl.BlockSpec((tm, D_MODEL), lambda j, i: (i, 0)),
            pl.BlockSpec((1, D_MODEL, tf), lambda j, i: (0, 0, j)),
            pl.BlockSpec((1, D_MODEL, tf), lambda j, i: (0, 0, j + nf)),
        ],
        out_specs=pl.BlockSpec((tm, tf), lambda j, i: (i, j)),
        out_shape=jax.ShapeDtypeStruct((t, D_FF), BF16),
        scratch_shapes=[pltpu.VMEM((D_MODEL, tf), BF16), pltpu.VMEM((D_MODEL, tf), BF16)],
        compiler_params=pltpu.CompilerParams(
            dimension_semantics=("parallel", "arbitrary"), vmem_limit_bytes=VMEM_LIMIT),
        name="ffn_up",
    )(u, w_in3, w_in3)


def _ffn_down_kernel(*refs, mod0, gidx, next_mod0, next_gidx, row_chunk):
    if next_mod0 is None:
        act_ref, h_ref, m_ref, g_ref, wo_ref, o_ref = refs
    else:
        act_ref, h_ref, m_ref, g_ref, wo_ref, wlo_ref, o_ref, u_ref, lo_ref = refs
    gate = m_ref[0, mod0 + 2:mod0 + 3, :]
    us = []
    for r in range(0, h_ref.shape[0], row_chunk):
        rows = slice(r, r + row_chunk)
        y = _dot(act_ref[rows, :], wo_ref[...])
        o = h_ref[rows, :] + MACARON_WEIGHT * gate * (_rms(y) * g_ref[gidx + 1:gidx + 2, :])
        o_ref[rows, :] = o
        if next_mod0 is not None:
            us.append(_mod_norm(o, m_ref, g_ref, next_mod0, next_gidx))
            u_ref[rows, :] = us[-1]
    if next_mod0 is not None:
        lo_ref[...] = _dot(jnp.concatenate(us, axis=0), wlo_ref[...]).astype(BF16)


def _ffn_down(act, h, m3, norm_g, w_out, w_lo, *, mod0, gidx, next_mod0, next_gidx, tm, rows_per_mod,
              fixed_row=None):
    t = h.shape[0]
    emit_next = next_mod0 is not None
    tok = lambda i: (i, 0)
    const = lambda i: (0, 0)
    in_specs = [
        pl.BlockSpec((tm, D_FF), tok),
        pl.BlockSpec((tm, D_MODEL), tok),
        pl.BlockSpec((1, N_MOD, D_MODEL), _mod_row_map(tm, rows_per_mod, fixed_row)),
        pl.BlockSpec((6, D_MODEL), const),
        pl.BlockSpec((D_FF, D_MODEL), const),
    ]
    out_specs = [pl.BlockSpec((tm, D_MODEL), tok)]
    out_shape = [jax.ShapeDtypeStruct((t, D_MODEL), F32)]
    args = [act, h, m3, norm_g, w_out]
    if emit_next:
        in_specs.append(pl.BlockSpec((D_MODEL, LO_PAD), const))
        out_specs += [pl.BlockSpec((tm, D_MODEL), tok), pl.BlockSpec((tm, LO_PAD), tok)]
        out_shape += [jax.ShapeDtypeStruct((t, D_MODEL), BF16), jax.ShapeDtypeStruct((t, LO_PAD), BF16)]
        args.append(w_lo)
    return pl.pallas_call(
        functools.partial(_ffn_down_kernel, mod0=mod0, gidx=gidx, next_mod0=next_mod0, next_gidx=next_gidx,
                          row_chunk=tm // 2),
        grid=(t // tm,),
        in_specs=in_specs,
        out_specs=out_specs,
        out_shape=out_shape,
        compiler_params=pltpu.CompilerParams(
            dimension_semantics=("parallel",), vmem_limit_bytes=BIG_VMEM_LIMIT),
        name="ffn_down",
    )(*args)


def _proj_kernel(u_ref, w_ref, wg_ref, p_ref, *, head_blocks):
    n = pl.program_id(0)

    @pl.when(n < head_blocks)
    def _():
        p_ref[...] = _dot(u_ref[...], w_ref[...]).astype(BF16)

    @pl.when(n >= head_blocks)
    def _():
        p_ref[...] = _dot(u_ref[...], wg_ref[...]).astype(BF16)


def _proj(u, w, w_gates, *, tm, first_block, head_blocks, n_blocks, rot):
    t = u.shape[0]
    tn = PROJ_TN
    return pl.pallas_call(
        functools.partial(_proj_kernel, head_blocks=head_blocks),
        grid=(n_blocks, t // tm),
        in_specs=[
            pl.BlockSpec((tm, D_MODEL), lambda n, i: (i, 0)),
            pl.BlockSpec((D_MODEL, tn), lambda n, i: (0, first_block + jnp.minimum(n, head_blocks - 1))),
            pl.BlockSpec((D_MODEL, tn), lambda n, i: (0, jnp.maximum(n - head_blocks, 0))),
        ],
        out_specs=pl.BlockSpec((tm, tn), lambda n, i: (i, (n + rot) % n_blocks)),
        out_shape=jax.ShapeDtypeStruct((t, n_blocks * tn), BF16),
        compiler_params=pltpu.CompilerParams(
            dimension_semantics=("parallel", "parallel"), vmem_limit_bytes=VMEM_LIMIT),
        name="proj",
    )(u, w, w_gates)


def _log2_decay(lo_bf16, gu, bias):
    z = _dot(lo_bf16, gu) + bias
    return (jnp.minimum(z, 0.0) - jnp.log(1.0 + jnp.exp(-jnp.abs(z)))) * (LOG2_E / GATE_TAU)


def _split2(x):
    hi = x.astype(BF16)
    lo = (x - hi.astype(F32)).astype(BF16)
    return jnp.concatenate([hi, lo], axis=0)


def _tri2(n, upper, strict):
    r = lax.broadcasted_iota(jnp.int32, (n, 2 * n), 0)
    c = lax.broadcasted_iota(jnp.int32, (n, 2 * n), 1) & (n - 1)
    if upper:
        m = (c > r) if strict else (c >= r)
    else:
        m = (c < r) if strict else (c <= r)
    return m.astype(BF16)


def _gla_kernel(q_ref, k_ref, v_ref, lo_ref, ck_ref, cv_ref, clo_ref, gu_ref, gb_ref, on_ref,
                qd_f, kd_f, qd_b, kd_b, o_acc, o_bwd, a_f, a_b, st_f, st_b):
    C = GLA_CHUNK
    n_chunks = SEQ // C
    n_pairs = n_chunks // 2
    gu = gu_ref[0]
    bias = gb_ref[0]

    la_c = _log2_decay(clo_ref[0], gu, bias)
    ck = ck_ref[0].astype(F32)
    cv = cv_ref[0]
    dec_f = jnp.exp2(_dot(_tri2(CTX_LEN, True, True), _split2(la_c[:, :HEAD_K])))
    dec_b = jnp.exp2(_dot(_tri2(CTX_LEN, False, True), _split2(la_c[:, HEAD_K:])))
    st_f[...] = _dot_t_lhs(cv, (ck * dec_f).astype(BF16))
    st_b[...] = _dot_t_lhs(cv, (ck * dec_b).astype(BF16))

    tr = lax.broadcasted_iota(jnp.int32, (2 * C, 2 * C), 0)
    tc = lax.broadcasted_iota(jnp.int32, (2 * C, 2 * C), 1) & (C - 1)
    tri_fb = (((tr < C) & (tc <= tr)) | ((tr >= C) & (tc >= tr - C))).astype(BF16)
    pr = lax.broadcasted_iota(jnp.int32, (C, 2 * C), 0)
    pc = lax.broadcasted_iota(jnp.int32, (C, 2 * C), 1)
    keep_f = pc <= pr
    keep_b = pc - C >= pr

    def prepare(i, carry):
        r0 = pl.multiple_of(i * (GLA_PREP_CHUNKS * C), GLA_PREP_CHUNKS * C)
        la = _log2_decay(lo_ref[0, pl.ds(r0, GLA_PREP_CHUNKS * C), :], gu, bias)
        cum = [_dot(tri_fb, _split2(la[j * C:(j + 1) * C, :])) for j in range(GLA_PREP_CHUNKS)]
        chunks = []
        for j in range(GLA_PREP_CHUNKS):
            rows = pl.ds(r0 + j * C, C)
            qc = q_ref[0, rows, :].astype(F32)
            kc = k_ref[0, rows, :].astype(F32)
            b_f = cum[j][:C, :HEAD_K]
            b_b = cum[j][C:, HEAD_K:]
            dirs = []
            for fwd, b in ((True, b_f), (False, b_b)):
                if fwd:
                    b_mid, b_last = b[C // 2 - 1:C // 2, :], b[C - 1:C, :]
                else:
                    b_mid, b_last = b[C // 2:C // 2 + 1, :], b[0:1, :]
                qs = qc * jnp.exp2(b - (b_mid + Q_SCALE_LOG2))
                ks = kc * jnp.exp2(b_mid - b)
                dirs.append(dict(qs=qs.astype(BF16), ks=ks.astype(BF16), qd=qs * jnp.exp2(b_mid),
                                 kd=ks * jnp.exp2(b_last - b_mid), a=jnp.exp2(b_last)))
            scores = _dot_t_rhs(jnp.concatenate([dirs[0]["qs"], dirs[1]["qs"]], axis=0),
                                jnp.concatenate([dirs[0]["ks"], dirs[1]["ks"]], axis=0))
            chunks.append(dict(rows=rows, f=dirs[0], b=dirs[1], scores=scores))
        crosses = []
        for p in range(GLA_PREP_CHUNKS // 2):
            lo_c, hi_c = chunks[2 * p], chunks[2 * p + 1]
            lo_f, hi_f, lo_b, hi_b = lo_c["f"], hi_c["f"], lo_c["b"], hi_c["b"]
            qd_f[lo_c["rows"], :] = lo_f["qd"].astype(BF16)
            qd_f[hi_c["rows"], :] = (hi_f["qd"] * lo_f["a"]).astype(BF16)
            kd_f[lo_c["rows"], :] = (lo_f["kd"] * hi_f["a"]).astype(BF16)
            kd_f[hi_c["rows"], :] = hi_f["kd"].astype(BF16)
            qd_b[hi_c["rows"], :] = hi_b["qd"].astype(BF16)
            qd_b[lo_c["rows"], :] = (lo_b["qd"] * hi_b["a"]).astype(BF16)
            kd_b[hi_c["rows"], :] = (hi_b["kd"] * lo_b["a"]).astype(BF16)
            kd_b[lo_c["rows"], :] = lo_b["kd"].astype(BF16)
            pair = pl.ds(i * (GLA_PREP_CHUNKS // 2) + p, 1)
            a_f[pair, :] = lo_f["a"] * hi_f["a"]
            a_b[pair, :] = lo_b["a"] * hi_b["a"]
            crosses.append(_dot_t_rhs(
                jnp.concatenate([hi_f["qd"].astype(BF16), lo_b["qd"].astype(BF16)], axis=0),
                jnp.concatenate([lo_f["kd"].astype(BF16), hi_b["kd"].astype(BF16)], axis=0)))
        for p in range(GLA_PREP_CHUNKS // 2):
            lo_c, hi_c = chunks[2 * p], chunks[2 * p + 1]
            v_lo = v_ref[0, lo_c["rows"], :]
            v_hi = v_ref[0, hi_c["rows"], :]
            zero = jnp.zeros((C, 2 * C), F32)
            w_lo = jnp.concatenate(
                [jnp.where(keep_f, lo_c["scores"][:C, :], jnp.where(keep_b, lo_c["scores"][C:, :], 0.0)),
                 jnp.where(pc >= C, crosses[p][C:, :], zero)], axis=1)
            o_acc[lo_c["rows"], :] = _dot(w_lo.astype(BF16), jnp.concatenate([v_lo, v_lo, v_lo, v_hi], axis=0))
            w_hi = jnp.concatenate(
                [jnp.where(keep_f, hi_c["scores"][:C, :], jnp.where(keep_b, hi_c["scores"][C:, :], 0.0)),
                 jnp.where(pc < C, crosses[p][:C, :], zero)], axis=1)
            o_acc[hi_c["rows"], :] = _dot(w_hi.astype(BF16), jnp.concatenate([v_hi, v_hi, v_lo, v_lo], axis=0))
        return carry

    def scan(i, carry, final):
        steps = ((i, qd_f, kd_f, a_f, st_f), (n_pairs - 1 - i, qd_b, kd_b, a_b, st_b))
        rows = [pl.ds(pl.multiple_of(c * (2 * C), 2 * C), 2 * C) for c, *_ in steps]
        incs = [_dot_t_lhs(v_ref[0, r, :], kd[r, :]) for r, (_, _, kd, _, _) in zip(rows, steps)]
        outs = [_dot_t_rhs(qd[r, :], st[...].astype(BF16)) for r, (_, qd, _, _, st) in zip(rows, steps)]
        if not final:
            o_acc[rows[0], :] += outs[0]
            upper = pl.ds(pl.multiple_of((n_pairs // 2 - 1 - i) * (2 * C), 2 * C), 2 * C)
            o_bwd[upper, :] = outs[1]
        else:
            upper = pl.ds(pl.multiple_of((i - n_pairs // 2) * (2 * C), 2 * C), 2 * C)
            on_ref[0, rows[0], :] = _rms(o_acc[rows[0], :] + o_bwd[upper, :] + outs[0]).astype(BF16)
            on_ref[0, rows[1], :] = _rms(o_acc[rows[1], :] + outs[1]).astype(BF16)
        for inc, (c, _, _, a, st) in zip(incs, steps):
            st[...] = st[...] * a[pl.ds(c, 1), :] + inc
        return carry

    lax.fori_loop(0, n_chunks // GLA_PREP_CHUNKS, prepare, 0)
    lax.fori_loop(0, n_pairs // 2, functools.partial(scan, final=False), 0, unroll=2)
    lax.fori_loop(n_pairs // 2, n_pairs, functools.partial(scan, final=True), 0, unroll=2)


def _gla(p3, lo3, cp3, clo3, gu, gbias):
    qb, kb, vb = COL_Q // HEAD_K, COL_K // HEAD_K, COL_V // HEAD_V
    n_chunks = SEQ // GLA_CHUNK
    return pl.pallas_call(
        _gla_kernel,
        grid=(BATCH, GLA_HEADS),
        in_specs=[
            pl.BlockSpec((1, SEQ, HEAD_K), lambda b, h: (b, 0, qb + h)),
            pl.BlockSpec((1, SEQ, HEAD_K), lambda b, h: (b, 0, kb + h)),
            pl.BlockSpec((1, SEQ, HEAD_V), lambda b, h: (b, 0, vb + h)),
            pl.BlockSpec((1, SEQ, LO_PAD), lambda b, h: (b, 0, 0)),
            pl.BlockSpec((1, CTX_LEN, HEAD_K), lambda b, h: (b, 0, h)),
            pl.BlockSpec((1, CTX_LEN, HEAD_V), lambda b, h: (b, 0, GLA_DK // HEAD_V + h)),
            pl.BlockSpec((1, CTX_LEN, LO_PAD), lambda b, h: (b, 0, 0)),
            pl.BlockSpec((1, LO_PAD, 2 * HEAD_K), lambda b, h: (h, 0, 0)),
            pl.BlockSpec((1, 1, 2 * HEAD_K), lambda b, h: (h, 0, 0)),
        ],
        out_specs=pl.BlockSpec((1, SEQ, HEAD_V), lambda b, h: (b, 0, h)),
        out_shape=jax.ShapeDtypeStruct((BATCH, SEQ, GLA_DV), BF16),
        scratch_shapes=[
            pltpu.VMEM((SEQ, HEAD_K), BF16),
            pltpu.VMEM((SEQ, HEAD_K), BF16),
            pltpu.VMEM((SEQ, HEAD_K), BF16),
            pltpu.VMEM((SEQ, HEAD_K), BF16),
            pltpu.VMEM((SEQ, HEAD_V), F32),
            pltpu.VMEM((SEQ // 2, HEAD_V), F32),
            pltpu.VMEM((n_chunks // 2, HEAD_K), F32),
            pltpu.VMEM((n_chunks // 2, HEAD_K), F32),
            pltpu.VMEM((HEAD_V, HEAD_K), F32),
            pltpu.VMEM((HEAD_V, HEAD_K), F32),
        ],
        compiler_params=pltpu.CompilerParams(
            dimension_semantics=("parallel", "arbitrary"), vmem_limit_bytes=BIG_VMEM_LIMIT),
        name="gla",
    )(p3, p3, p3, lo3, cp3, cp3, clo3, gu, gbias)


def _merge_kernel(bg_ref, cg_ref, hv_ref, r_ref, ga_ref, gb_ref, on_ref, x_ref, m_ref, g_ref,
                  cw_ref, cb_ref, gng_ref, wc_ref, wg_ref, wo_ref, o_ref, u_ref):
    tm = x_ref.shape[0]
    z = cg_ref[...].astype(F32) * hv_ref[...].astype(F32)
    col = lax.broadcasted_iota(jnp.int32, (tm, 1), 0) & (GRID_W - 1)
    z_prev = jnp.where(col == 0, 0.0, pltpu.roll(z, 1, 0))
    z_next = jnp.where(col == GRID_W - 1, 0.0, pltpu.roll(z, tm - 1, 0))
    y = z_prev * cw_ref[0:1, :] + z * cw_ref[1:2, :] + z_next * cw_ref[2:3, :] + cb_ref[...]
    y_a = _dot((bg_ref[...].astype(F32) * y).astype(BF16), wc_ref[...])
    rd = on_ref[...].astype(F32) * gng_ref[...] * _silu(r_ref[...].astype(F32))
    y_b = _dot(rd.astype(BF16), wg_ref[...])
    mix = (jax.nn.sigmoid(ga_ref[...].astype(F32)) * y_a
           + jax.nn.sigmoid(gb_ref[...].astype(F32)) * y_b)
    yo = _dot(mix.astype(BF16), wo_ref[...])
    gate = m_ref[0, 5:6, :]
    o = x_ref[...] + gate * (_rms(yo) * g_ref[3:4, :])
    o_ref[...] = o
    u_ref[...] = _mod_norm(o, m_ref, g_ref, 6, 4)


def _merge(p, on, x1, m3, norm_g, conv_w, conv_b, gng, wc, wg, wo, *, tm):
    t = x1.shape[0]
    tiles_per_mod = SEQ // tm
    tok = lambda blk: (lambda i: (i, blk))
    const2 = lambda i: (0, 0)
    return pl.pallas_call(
        _merge_kernel,
        grid=(t // tm,),
        in_specs=[
            pl.BlockSpec((tm, CONV_W), tok(COL_BG // CONV_W)),
            pl.BlockSpec((tm, CONV_W), tok(COL_CG // CONV_W)),
            pl.BlockSpec((tm, CONV_W), tok(COL_HV // CONV_W)),
            pl.BlockSpec((tm, GLA_DV), tok(COL_R // GLA_DV)),
            pl.BlockSpec((tm, D_MODEL), tok(COL_GA // D_MODEL)),
            pl.BlockSpec((tm, D_MODEL), tok(COL_GB // D_MODEL)),
            pl.BlockSpec((tm, GLA_DV), tok(0)),
            pl.BlockSpec((tm, D_MODEL), tok(0)),
            pl.BlockSpec((1, N_MOD, D_MODEL), lambda i: (i // tiles_per_mod, 0, 0)),
            pl.BlockSpec((6, D_MODEL), const2),
            pl.BlockSpec((3, CONV_W), const2),
            pl.BlockSpec((1, CONV_W), const2),
            pl.BlockSpec((1, GLA_DV), const2),
            pl.BlockSpec((CONV_W, D_MODEL), const2),
            pl.BlockSpec((GLA_DV, D_MODEL), const2),
            pl.BlockSpec((D_MODEL, D_MODEL), const2),
        ],
        out_specs=[pl.BlockSpec((tm, D_MODEL), tok(0)), pl.BlockSpec((tm, D_MODEL), tok(0))],
        out_shape=[jax.ShapeDtypeStruct((t, D_MODEL), F32), jax.ShapeDtypeStruct((t, D_MODEL), BF16)],
        compiler_params=pltpu.CompilerParams(
            dimension_semantics=("parallel",), vmem_limit_bytes=VMEM_LIMIT),
        name="merge",
    )(p, p, p, p, p, p, on, x1, m3, norm_g, conv_w, conv_b, gng, wc, wg, wo)


def kernel(x, c, ctx, c_ctx, w_mod, b_mod, norm_g, ffn1_w_in, ffn1_w_out, w_in, conv_w, conv_b,
           conv_out, gate_up, gate_bias, gla_norm_g, gla_out, w_o, ffn2_w_in, ffn2_w_out):
    assert w_mod.shape[0] == 1, "single-layer block"
    t = BATCH * SEQ
    tc = BATCH * CTX_LEN

    lo0 = PROJ_HEAD_BLOCKS * PROJ_TN
    w = w_in[0].astype(BF16)
    w_gates = w[:, lo0 + 2 * GATE_RANK:]
    w_lo = jnp.pad(w[:, lo0:lo0 + 2 * GATE_RANK], ((0, 0), (0, LO_PAD - 2 * GATE_RANK)))
    per_head = lambda a: a.reshape(GATE_RANK, GLA_HEADS, HEAD_K).transpose(1, 0, 2)
    gu = jnp.zeros((GLA_HEADS, LO_PAD, 2 * HEAD_K), F32)
    gu = gu.at[:, 0:GATE_RANK, :HEAD_K].set(per_head(gate_up[0, 0]))
    gu = gu.at[:, GATE_RANK:2 * GATE_RANK, HEAD_K:].set(per_head(gate_up[0, 1]))
    gu = gu.astype(BF16)
    gbias = jnp.concatenate([gate_bias[0, 0].reshape(GLA_HEADS, 1, HEAD_K),
                             gate_bias[0, 1].reshape(GLA_HEADS, 1, HEAD_K)], axis=2)
    f1_in, f1_out = ffn1_w_in, ffn1_w_out[0].astype(BF16)
    f2_in, f2_out = ffn2_w_in, ffn2_w_out[0].astype(BF16)
    wc, wg, wo = conv_out[0].astype(BF16), gla_out[0].astype(BF16), w_o[0].astype(BF16)
    g = norm_g[0]

    cc = jnp.concatenate([c, c_ctx[None, :], jnp.zeros((MOD_ROWS - BATCH - 1, D_MODEL), F32)], axis=0)
    m3 = _modulation(cc, w_mod[0], b_mod).reshape(MOD_ROWS, N_MOD, D_MODEL)

    xf = x.reshape(t, D_MODEL)
    cf = ctx.reshape(tc, D_MODEL)

    u1 = _prenorm(xf, m3, g, mod0=0, gidx=0, tm=1024, rows_per_mod=SEQ)
    x1, u2, lo = _ffn_down(_ffn_up(u1, f1_in, tm=2048, tf=512), xf, m3, g, f1_out, w_lo, mod0=0, gidx=0,
                           next_mod0=3, next_gidx=2, tm=512, rows_per_mod=SEQ)
    cu1 = _prenorm(cf, m3, g, mod0=0, gidx=0, tm=512, rows_per_mod=CTX_LEN, fixed_row=CTX_ROW)
    _, cu2, clo = _ffn_down(_ffn_up(cu1, f1_in, tm=1024, tf=512), cf, m3, g, f1_out, w_lo, mod0=0, gidx=0,
                            next_mod0=3, next_gidx=2, tm=256, rows_per_mod=CTX_LEN, fixed_row=CTX_ROW)

    p = _proj(u2, w, w_gates, tm=2048, first_block=0, head_blocks=PROJ_HEAD_BLOCKS, n_blocks=PROJ_BLOCKS,
              rot=PROJ_ROT)
    cp = _proj(cu2, w, w_gates, tm=1024, first_block=PROJ_K_BLOCK, head_blocks=3, n_blocks=3, rot=0)

    on = _gla(p.reshape(BATCH, SEQ, P_COLS), lo.reshape(BATCH, SEQ, LO_PAD),
              cp.reshape(BATCH, CTX_LEN, GLA_DK + GLA_DV), clo.reshape(BATCH, CTX_LEN, LO_PAD),
              gu, gbias)

    x2, u3 = _merge(p, on.reshape(t, GLA_DV), x1, m3, g, conv_w[0], conv_b, gla_norm_g, wc, wg, wo, tm=256)
    out = _ffn_down(_ffn_up(u3, f2_in, tm=2048, tf=512), x2, m3, g, f2_out, None, mod0=6, gidx=4,
                    next_mod0=None, next_gidx=None, tm=512, rows_per_mod=SEQ)[0]
    return out.reshape(BATCH, SEQ, D_MODEL)
```

```python
import functools

import jax
import jax.numpy as jnp
from jax import lax
from jax.experimental import pallas as pl
from jax.experimental.pallas import tpu as pltpu

F32 = jnp.float32
BF16 = jnp.bfloat16

D_MODEL = 2048
BATCH = 4
SEQ = 4096
GRID_W = 64
CTX_LEN = 256
D_FF = 5632
MACARON_WEIGHT = 0.5
CONV_W = 1024
GLA_HEADS = 4
GLA_DK = 1024
GLA_DV = 2048
HEAD_K = GLA_DK // GLA_HEADS
HEAD_V = GLA_DV // GLA_HEADS
GATE_RANK = 16
GATE_TAU = 16.0
N_MOD = 9
EPS = 1e-6
LOG2_E = 1.4426950408889634
Q_SCALE_LOG2 = 4.0
assert 2.0 ** (2 * Q_SCALE_LOG2) == HEAD_K

LANES = 128
MOD_ROWS = 8
CTX_ROW = BATCH
GLA_CHUNK = 64
GLA_PREP_CHUNKS = 8
FFN_UP_ROW_CHUNK = 256
LO_PAD = LANES

PROJ_TN = 1024
PROJ_HEAD_BLOCKS = 9
PROJ_BLOCKS = 13
PROJ_ROT = 8
PROJ_K_BLOCK = 4
PROJ_SIDE_SLABS = 64
COL_V, COL_R, COL_GA, COL_GB = 0, 2048, 4096, 6144
COL_BG, COL_CG, COL_HV, COL_Q, COL_K = 8192, 9216, 10240, 11264, 12288
P_COLS = PROJ_BLOCKS * PROJ_TN

VMEM_LIMIT = 56 * 1024 * 1024
BIG_VMEM_LIMIT = 60 * 1024 * 1024


def _dot(a, b):
    return jnp.dot(a, b, preferred_element_type=F32)


def _dot_t_lhs(a, b):
    return lax.dot_general(a, b, (((0,), (0,)), ((), ())), preferred_element_type=F32)


def _dot_t_rhs(a, b):
    return lax.dot_general(a, b, (((1,), (1,)), ((), ())), preferred_element_type=F32)


def _rms(x):
    return x * lax.rsqrt(jnp.mean(x * x, axis=-1, keepdims=True) + EPS)


def _silu(x):
    return x * jax.nn.sigmoid(x)


def _mod_kernel(c_ref, w_ref, b_ref, o_ref):
    s = _silu(c_ref[...]).astype(BF16)
    o_ref[...] = _dot(s, w_ref[...].astype(BF16)) + b_ref[...]


def _modulation(cc, w_mod, b_mod):
    n = w_mod.shape[1]
    tn = 1024
    return pl.pallas_call(
        _mod_kernel,
        grid=(n // tn,),
        in_specs=[
            pl.BlockSpec((MOD_ROWS, D_MODEL), lambda j: (0, 0)),
            pl.BlockSpec((D_MODEL, tn), lambda j: (0, j)),
            pl.BlockSpec((1, tn), lambda j: (0, j)),
        ],
        out_specs=pl.BlockSpec((MOD_ROWS, tn), lambda j: (0, j)),
        out_shape=jax.ShapeDtypeStruct((MOD_ROWS, n), F32),
        compiler_params=pltpu.CompilerParams(
            dimension_semantics=("arbitrary",), vmem_limit_bytes=VMEM_LIMIT),
        name="modulation",
    )(cc, w_mod, b_mod)


def _mod_norm(h, m_ref, g_ref, mod0, gidx):
    shift = m_ref[0, mod0:mod0 + 1, :]
    gain = g_ref[gidx:gidx + 1, :] * (1.0 + m_ref[0, mod0 + 1:mod0 + 2, :])
    return (_rms(h) * gain + shift).astype(BF16)


def _mod_row_map(tm, rows_per_mod, fixed_row):
    if fixed_row is not None:
        return lambda i, *_: (fixed_row, 0, 0)
    tiles_per_mod = rows_per_mod // tm
    return lambda i, *_: (i // tiles_per_mod, 0, 0)


def _prenorm_kernel(h_ref, m_ref, g_ref, u_ref, *, mod0, gidx):
    u_ref[...] = _mod_norm(h_ref[...], m_ref, g_ref, mod0, gidx)


def _prenorm(h, m3, norm_g, *, mod0, gidx, tm, rows_per_mod, fixed_row=None):
    t = h.shape[0]
    return pl.pallas_call(
        functools.partial(_prenorm_kernel, mod0=mod0, gidx=gidx),
        grid=(t // tm,),
        in_specs=[
            pl.BlockSpec((tm, D_MODEL), lambda i: (i, 0)),
            pl.BlockSpec((1, N_MOD, D_MODEL), _mod_row_map(tm, rows_per_mod, fixed_row)),
            pl.BlockSpec((6, D_MODEL), lambda i: (0, 0)),
        ],
        out_specs=pl.BlockSpec((tm, D_MODEL), lambda i: (i, 0)),
        out_shape=jax.ShapeDtypeStruct((t, D_MODEL), BF16),
        compiler_params=pltpu.CompilerParams(
            dimension_semantics=("parallel",), vmem_limit_bytes=VMEM_LIMIT),
        name="prenorm",
    )(h, m3, norm_g)


def _ffn_up_kernel(*refs, cast_w_out):
    if cast_w_out:
        u_ref, wa_ref, wb_ref, wo_ref, act_ref, wo_bf_ref, wa_bf, wb_bf = refs
        wo_bf_ref[...] = wo_ref[0].astype(BF16)
    else:
        u_ref, wa_ref, wb_ref, act_ref, wa_bf, wb_bf = refs

    @pl.when(pl.program_id(1) == 0)
    def _():
        wa_bf[...] = wa_ref[0].astype(BF16)
        wb_bf[...] = wb_ref[0].astype(BF16)

    for r in range(0, u_ref.shape[0], FFN_UP_ROW_CHUNK):
        u = u_ref[r:r + FFN_UP_ROW_CHUNK, :]
        a = _dot(u, wa_bf[...])
        b = _dot(u, wb_bf[...])
        act_ref[r:r + FFN_UP_ROW_CHUNK, :] = (_silu(b) * a).astype(BF16)


def _ffn_up(u, w_in3, w_out3=None, *, tm, tf):
    t = u.shape[0]
    nf = D_FF // tf
    n_i = t // tm
    in_specs = [
        pl.BlockSpec((tm, D_MODEL), lambda j, i: (i, 0)),
        pl.BlockSpec((1, D_MODEL, tf), lambda j, i: (0, 0, j)),
        pl.BlockSpec((1, D_MODEL, tf), lambda j, i: (0, 0, j + nf)),
    ]
    out_specs = [pl.BlockSpec((tm, tf), lambda j, i: (i, j))]
    out_shape = [jax.ShapeDtypeStruct((t, D_FF), BF16)]
    args = [u, w_in3, w_in3]
    if w_out3 is not None:
        slab = D_FF // (nf * n_i)
        assert slab * nf * n_i == D_FF and slab % 16 == 0, "one bf16-tileable row slab of w_out per grid step"
        in_specs.append(pl.BlockSpec((1, slab, D_MODEL), lambda j, i: (0, j * n_i + i, 0)))
        out_specs.append(pl.BlockSpec((slab, D_MODEL), lambda j, i: (j * n_i + i, 0)))
        out_shape.append(jax.ShapeDtypeStruct((D_FF, D_MODEL), BF16))
        args.append(w_out3)
    return pl.pallas_call(
        functools.partial(_ffn_up_kernel, cast_w_out=w_out3 is not None),
        grid=(nf, n_i),
        in_specs=in_specs,
        out_specs=out_specs,
        out_shape=out_shape,
        scratch_shapes=[pltpu.VMEM((D_MODEL, tf), BF16), pltpu.VMEM((D_MODEL, tf), BF16)],
        compiler_params=pltpu.CompilerParams(
            dimension_semantics=("parallel", "arbitrary"), vmem_limit_bytes=VMEM_LIMIT),
        name="ffn_up",
    )(*args)


def _ffn_down_kernel(*refs, mod0, gidx, next_mod0, next_gidx, row_chunk):
    if next_mod0 is None:
        act_ref, h_ref, m_ref, g_ref, wo_ref, o_ref = refs
    else:
        act_ref, h_ref, m_ref, g_ref, wo_ref, wlo_ref, o_ref, u_ref, lo_ref = refs
    gain = (MACARON_WEIGHT * m_ref[0, mod0 + 2:mod0 + 3, :]) * g_ref[gidx + 1:gidx + 2, :]
    us = []
    for r in range(0, h_ref.shape[0], row_chunk):
        rows = slice(r, r + row_chunk)
        y = _dot(act_ref[rows, :], wo_ref[...])
        o = h_ref[rows, :] + _rms(y) * gain
        o_ref[rows, :] = o
        if next_mod0 is not None:
            us.append(_mod_norm(o, m_ref, g_ref, next_mod0, next_gidx))
            u_ref[rows, :] = us[-1]
    if next_mod0 is not None:
        lo_ref[...] = _dot(jnp.concatenate(us, axis=0), wlo_ref[...]).astype(BF16)


def _ffn_down(act, h, m3, norm_g, w_out, w_lo, *, mod0, gidx, next_mod0, next_gidx, tm, rows_per_mod,
              fixed_row=None):
    t = h.shape[0]
    emit_next = next_mod0 is not None
    tok = lambda i: (i, 0)
    const = lambda i: (0, 0)
    in_specs = [
        pl.BlockSpec((tm, D_FF), tok),
        pl.BlockSpec((tm, D_MODEL), tok),
        pl.BlockSpec((1, N_MOD, D_MODEL), _mod_row_map(tm, rows_per_mod, fixed_row)),
        pl.BlockSpec((6, D_MODEL), const),
        pl.BlockSpec((D_FF, D_MODEL), const),
    ]
    out_specs = [pl.BlockSpec((tm, D_MODEL), tok)]
    out_shape = [jax.ShapeDtypeStruct((t, D_MODEL), F32)]
    args = [act, h, m3, norm_g, w_out]
    if emit_next:
        in_specs.append(pl.BlockSpec((D_MODEL, LO_PAD), const))
        out_specs += [pl.BlockSpec((tm, D_MODEL), tok), pl.BlockSpec((tm, LO_PAD), tok)]
        out_shape += [jax.ShapeDtypeStruct((t, D_MODEL), BF16), jax.ShapeDtypeStruct((t, LO_PAD), BF16)]
        args.append(w_lo)
    return pl.pallas_call(
        functools.partial(_ffn_down_kernel, mod0=mod0, gidx=gidx, next_mod0=next_mod0, next_gidx=next_gidx,
                          row_chunk=tm // 2),
        grid=(t // tm,),
        in_specs=in_specs,
        out_specs=out_specs,
        out_shape=out_shape,
        compiler_params=pltpu.CompilerParams(
            dimension_semantics=("parallel",), vmem_limit_bytes=BIG_VMEM_LIMIT),
        name="ffn_down",
    )(*args)


def _proj_kernel(*refs, head_blocks, n_side):
    u_ref, w_ref, wg_ref = refs[:3]
    side_in = refs[3:3 + n_side]
    p_ref = refs[3 + n_side]
    side_out = refs[4 + n_side:]
    n = pl.program_id(0)

    for src, dst in zip(side_in, side_out):
        dst[...] = src[0].astype(BF16)

    @pl.when(n < head_blocks)
    def _():
        p_ref[...] = _dot(u_ref[...], w_ref[...]).astype(BF16)

    @pl.when(n >= head_blocks)
    def _():
        p_ref[...] = _dot(u_ref[...], wg_ref[...]).astype(BF16)


def _proj(u, w, w_gates, side=(), *, tm, first_block, head_blocks, n_blocks, rot):
    t = u.shape[0]
    tn = PROJ_TN
    n_i = t // tm
    assert not side or n_blocks * n_i >= PROJ_SIDE_SLABS
    slab_idx = lambda n, i: jnp.minimum(n * n_i + i, PROJ_SIDE_SLABS - 1)
    in_specs = [
        pl.BlockSpec((tm, D_MODEL), lambda n, i: (i, 0)),
        pl.BlockSpec((D_MODEL, tn), lambda n, i: (0, first_block + jnp.minimum(n, head_blocks - 1))),
        pl.BlockSpec((D_MODEL, tn), lambda n, i: (0, jnp.maximum(n - head_blocks, 0))),
    ]
    out_specs = [pl.BlockSpec((tm, tn), lambda n, i: (i, (n + rot) % n_blocks))]
    out_shape = [jax.ShapeDtypeStruct((t, n_blocks * tn), BF16)]
    for a in side:
        rows, cols = a.shape[1] // PROJ_SIDE_SLABS, a.shape[2]
        assert rows * PROJ_SIDE_SLABS == a.shape[1] and rows % 16 == 0, "bf16-tileable row slabs"
        in_specs.append(pl.BlockSpec((1, rows, cols), lambda n, i: (0, slab_idx(n, i), 0)))
        out_specs.append(pl.BlockSpec((rows, cols), lambda n, i: (slab_idx(n, i), 0)))
        out_shape.append(jax.ShapeDtypeStruct(a.shape[1:], BF16))
    return pl.pallas_call(
        functools.partial(_proj_kernel, head_blocks=head_blocks, n_side=len(side)),
        grid=(n_blocks, n_i),
        in_specs=in_specs,
        out_specs=out_specs,
        out_shape=out_shape,
        compiler_params=pltpu.CompilerParams(
            dimension_semantics=("arbitrary", "arbitrary"), vmem_limit_bytes=VMEM_LIMIT),
        name="proj",
    )(u, w, w_gates, *side)


def _log2_decay(lo_bf16, gu, bias):
    z = _dot(lo_bf16, gu) + bias
    return (jnp.minimum(z, 0.0) - jnp.log(1.0 + jnp.exp(-jnp.abs(z)))) * (LOG2_E / GATE_TAU)


def _split2(x):
    hi = x.astype(BF16)
    lo = (x - hi.astype(F32)).astype(BF16)
    return jnp.concatenate([hi, lo], axis=0)


def _tri2(n, upper, strict):
    r = lax.broadcasted_iota(jnp.int32, (n, 2 * n), 0)
    c = lax.broadcasted_iota(jnp.int32, (n, 2 * n), 1) & (n - 1)
    if upper:
        m = (c > r) if strict else (c >= r)
    else:
        m = (c < r) if strict else (c <= r)
    return m.astype(BF16)


def _gla_kernel(q_ref, k_ref, v_ref, lo_ref, ck_ref, cv_ref, clo_ref, gu_ref, gb_ref, on_ref,
                qd_f, kd_f, qd_b, kd_b, o_acc, o_bwd, a_f, a_b, st_f, st_b):
    C = GLA_CHUNK
    n_chunks = SEQ // C
    n_pairs = n_chunks // 2
    gu = gu_ref[0]
    bias = gb_ref[0]

    la_c = _log2_decay(clo_ref[0], gu, bias)
    ck = ck_ref[0].astype(F32)
    cv = cv_ref[0]
    dec_f = jnp.exp2(_dot(_tri2(CTX_LEN, True, True), _split2(la_c[:, :HEAD_K])))
    dec_b = jnp.exp2(_dot(_tri2(CTX_LEN, False, True), _split2(la_c[:, HEAD_K:])))
    st_f[...] = _dot_t_lhs(cv, (ck * dec_f).astype(BF16))
    st_b[...] = _dot_t_lhs(cv, (ck * dec_b).astype(BF16))

    tr = lax.broadcasted_iota(jnp.int32, (2 * C, 2 * C), 0)
    tc = lax.broadcasted_iota(jnp.int32, (2 * C, 2 * C), 1) & (C - 1)
    tri_fb = (((tr < C) & (tc <= tr)) | ((tr >= C) & (tc >= tr - C))).astype(BF16)
    pr = lax.broadcasted_iota(jnp.int32, (C, 2 * C), 0)
    pc = lax.broadcasted_iota(jnp.int32, (C, 2 * C), 1)
    keep_f = pc <= pr
    keep_b = pc - C >= pr

    def prepare(i, carry):
        r0 = pl.multiple_of(i * (GLA_PREP_CHUNKS * C), GLA_PREP_CHUNKS * C)
        la = _log2_decay(lo_ref[0, pl.ds(r0, GLA_PREP_CHUNKS * C), :], gu, bias)
        cum = [_dot(tri_fb, _split2(la[j * C:(j + 1) * C, :])) for j in range(GLA_PREP_CHUNKS)]
        chunks = []
        for j in range(GLA_PREP_CHUNKS):
            rows = pl.ds(r0 + j * C, C)
            qc = q_ref[0, rows, :].astype(F32)
            kc = k_ref[0, rows, :].astype(F32)
            b_f = cum[j][:C, :HEAD_K]
            b_b = cum[j][C:, HEAD_K:]
            dirs = []
            for fwd, b in ((True, b_f), (False, b_b)):
                if fwd:
                    b_mid, b_last = b[C // 2 - 1:C // 2, :], b[C - 1:C, :]
                else:
                    b_mid, b_last = b[C // 2:C // 2 + 1, :], b[0:1, :]
                qs = qc * jnp.exp2(b - (b_mid + Q_SCALE_LOG2))
                ks = kc * jnp.exp2(b_mid - b)
                dirs.append(dict(qs=qs.astype(BF16), ks=ks.astype(BF16), qd=qs * jnp.exp2(b_mid),
                                 kd=ks * jnp.exp2(b_last - b_mid), a=jnp.exp2(b_last)))
            scores = _dot_t_rhs(jnp.concatenate([dirs[0]["qs"], dirs[1]["qs"]], axis=0),
                                jnp.concatenate([dirs[0]["ks"], dirs[1]["ks"]], axis=0))
            chunks.append(dict(rows=rows, f=dirs[0], b=dirs[1], scores=scores))
        crosses = []
        for p in range(GLA_PREP_CHUNKS // 2):
            lo_c, hi_c = chunks[2 * p], chunks[2 * p + 1]
            lo_f, hi_f, lo_b, hi_b = lo_c["f"], hi_c["f"], lo_c["b"], hi_c["b"]
            qd_f[lo_c["rows"], :] = lo_f["qd"].astype(BF16)
            qd_f[hi_c["rows"], :] = (hi_f["qd"] * lo_f["a"]).astype(BF16)
            kd_f[lo_c["rows"], :] = (lo_f["kd"] * hi_f["a"]).astype(BF16)
            kd_f[hi_c["rows"], :] = hi_f["kd"].astype(BF16)
            qd_b[hi_c["rows"], :] = hi_b["qd"].astype(BF16)
            qd_b[lo_c["rows"], :] = (lo_b["qd"] * hi_b["a"]).astype(BF16)
            kd_b[hi_c["rows"], :] = (hi_b["kd"] * lo_b["a"]).astype(BF16)
            kd_b[lo_c["rows"], :] = lo_b["kd"].astype(BF16)
            pair = pl.ds(i * (GLA_PREP_CHUNKS // 2) + p, 1)
            a_f[pair, :] = lo_f["a"] * hi_f["a"]
            a_b[pair, :] = lo_b["a"] * hi_b["a"]
            crosses.append(_dot_t_rhs(
                jnp.concatenate([hi_f["qd"].astype(BF16), lo_b["qd"].astype(BF16)], axis=0),
                jnp.concatenate([lo_f["kd"].astype(BF16), hi_b["kd"].astype(BF16)], axis=0)))
        for p in range(GLA_PREP_CHUNKS // 2):
            lo_c, hi_c = chunks[2 * p], chunks[2 * p + 1]
            v_lo = v_ref[0, lo_c["rows"], :]
            v_hi = v_ref[0, hi_c["rows"], :]
            zero = jnp.zeros((C, 2 * C), F32)
            w_lo = jnp.concatenate(
                [jnp.where(keep_f, lo_c["scores"][:C, :], jnp.where(keep_b, lo_c["scores"][C:, :], 0.0)),
                 jnp.where(pc >= C, crosses[p][C:, :], zero)], axis=1)
            o_acc[lo_c["rows"], :] = _dot(w_lo.astype(BF16), jnp.concatenate([v_lo, v_lo, v_lo, v_hi], axis=0))
            w_hi = jnp.concatenate(
                [jnp.where(keep_f, hi_c["scores"][:C, :], jnp.where(keep_b, hi_c["scores"][C:, :], 0.0)),
                 jnp.where(pc < C, crosses[p][:C, :], zero)], axis=1)
            o_acc[hi_c["rows"], :] = _dot(w_hi.astype(BF16), jnp.concatenate([v_hi, v_hi, v_lo, v_lo], axis=0))
        return carry

    def scan(i, carry, final):
        steps = ((i, qd_f, kd_f, a_f, st_f), (n_pairs - 1 - i, qd_b, kd_b, a_b, st_b))
        rows = [pl.ds(pl.multiple_of(c * (2 * C), 2 * C), 2 * C) for c, *_ in steps]
        incs = [_dot_t_lhs(v_ref[0, r, :], kd[r, :]) for r, (_, _, kd, _, _) in zip(rows, steps)]
        outs = [_dot_t_rhs(qd[r, :], st[...].astype(BF16)) for r, (_, qd, _, _, st) in zip(rows, steps)]
        if not final:
            o_acc[rows[0], :] += outs[0]
            upper = pl.ds(pl.multiple_of((n_pairs // 2 - 1 - i) * (2 * C), 2 * C), 2 * C)
            o_bwd[upper, :] = outs[1]
        else:
            upper = pl.ds(pl.multiple_of((i - n_pairs // 2) * (2 * C), 2 * C), 2 * C)
            on_ref[0, rows[0], :] = _rms(o_acc[rows[0], :] + o_bwd[upper, :] + outs[0]).astype(BF16)
            on_ref[0, rows[1], :] = _rms(o_acc[rows[1], :] + outs[1]).astype(BF16)
        for inc, (c, _, _, a, st) in zip(incs, steps):
            st[...] = st[...] * a[pl.ds(c, 1), :] + inc
        return carry

    lax.fori_loop(0, n_chunks // GLA_PREP_CHUNKS, prepare, 0)
    lax.fori_loop(0, n_pairs // 2, functools.partial(scan, final=False), 0, unroll=2)
    lax.fori_loop(n_pairs // 2, n_pairs, functools.partial(scan, final=True), 0, unroll=2)


def _gla(p3, lo3, cp3, clo3, gu, gbias):
    qb, kb, vb = COL_Q // HEAD_K, COL_K // HEAD_K, COL_V // HEAD_V
    n_chunks = SEQ // GLA_CHUNK
    return pl.pallas_call(
        _gla_kernel,
        grid=(BATCH, GLA_HEADS),
        in_specs=[
            pl.BlockSpec((1, SEQ, HEAD_K), lambda b, h: (b, 0, qb + h)),
            pl.BlockSpec((1, SEQ, HEAD_K), lambda b, h: (b, 0, kb + h)),
            pl.BlockSpec((1, SEQ, HEAD_V), lambda b, h: (b, 0, vb + h)),
            pl.BlockSpec((1, SEQ, LO_PAD), lambda b, h: (b, 0, 0)),
            pl.BlockSpec((1, CTX_LEN, HEAD_K), lambda b, h: (b, 0, h)),
            pl.BlockSpec((1, CTX_LEN, HEAD_V), lambda b, h: (b, 0, GLA_DK // HEAD_V + h)),
            pl.BlockSpec((1, CTX_LEN, LO_PAD), lambda b, h: (b, 0, 0)),
            pl.BlockSpec((1, LO_PAD, 2 * HEAD_K), lambda b, h: (h, 0, 0)),
            pl.BlockSpec((1, 1, 2 * HEAD_K), lambda b, h: (h, 0, 0)),
        ],
        out_specs=pl.BlockSpec((1, SEQ, HEAD_V), lambda b, h: (b, 0, h)),
        out_shape=jax.ShapeDtypeStruct((BATCH, SEQ, GLA_DV), BF16),
        scratch_shapes=[
            pltpu.VMEM((SEQ, HEAD_K), BF16),
            pltpu.VMEM((SEQ, HEAD_K), BF16),
            pltpu.VMEM((SEQ, HEAD_K), BF16),
            pltpu.VMEM((SEQ, HEAD_K), BF16),
            pltpu.VMEM((SEQ, HEAD_V), F32),
            pltpu.VMEM((SEQ // 2, HEAD_V), F32),
            pltpu.VMEM((n_chunks // 2, HEAD_K), F32),
            pltpu.VMEM((n_chunks // 2, HEAD_K), F32),
            pltpu.VMEM((HEAD_V, HEAD_K), F32),
            pltpu.VMEM((HEAD_V, HEAD_K), F32),
        ],
        compiler_params=pltpu.CompilerParams(
            dimension_semantics=("parallel", "arbitrary"), vmem_limit_bytes=BIG_VMEM_LIMIT),
        name="gla",
    )(p3, p3, p3, lo3, cp3, cp3, clo3, gu, gbias)


def _mix_kernel(bg_ref, cg_ref, hv_ref, r_ref, ga_ref, gb_ref, on_ref, cw_ref, cb_ref, gng_ref, wc_ref, wg_ref,
                mix_ref, *, row_chunk):
    chunks = [slice(r, r + row_chunk) for r in range(0, mix_ref.shape[0], row_chunk)]
    col = lax.broadcasted_iota(jnp.int32, (row_chunk, 1), 0) & (GRID_W - 1)
    branches = []
    for rows in chunks:
        z = cg_ref[rows, :].astype(F32) * hv_ref[rows, :].astype(F32)
        z_prev = jnp.where(col == 0, 0.0, pltpu.roll(z, 1, 0))
        z_next = jnp.where(col == GRID_W - 1, 0.0, pltpu.roll(z, row_chunk - 1, 0))
        y = z_prev * cw_ref[0:1, :] + z * cw_ref[1:2, :] + z_next * cw_ref[2:3, :] + cb_ref[...]
        y_a = _dot((bg_ref[rows, :].astype(F32) * y).astype(BF16), wc_ref[...])
        rd = on_ref[rows, :].astype(F32) * gng_ref[...] * _silu(r_ref[rows, :].astype(F32))
        branches.append((y_a, _dot(rd.astype(BF16), wg_ref[...])))
    for rows, (y_a, y_b) in zip(chunks, branches):
        mix_ref[rows, :] = (jax.nn.sigmoid(ga_ref[rows, :].astype(F32)) * y_a
                            + jax.nn.sigmoid(gb_ref[rows, :].astype(F32)) * y_b).astype(BF16)


def _mix_out_kernel(mix_ref, x_ref, m_ref, g_ref, wo_ref, o_ref, u_ref, *, row_chunk):
    gain = m_ref[0, 5:6, :] * g_ref[3:4, :]
    for r in range(0, x_ref.shape[0], row_chunk):
        rows = slice(r, r + row_chunk)
        o = x_ref[rows, :] + _rms(_dot(mix_ref[rows, :], wo_ref[...])) * gain
        o_ref[rows, :] = o
        u_ref[rows, :] = _mod_norm(o, m_ref, g_ref, 6, 4)


def _merge(p, on, x1, m3, norm_g, conv_w, conv_b, gng, wc, wg, wo, *, tm):
    t = x1.shape[0]
    tiles_per_mod = SEQ // tm
    tok = lambda blk: (lambda i: (i, blk))
    const2 = lambda i: (0, 0)
    mix = pl.pallas_call(
        functools.partial(_mix_kernel, row_chunk=tm // 2),
        grid=(t // tm,),
        in_specs=[
            pl.BlockSpec((tm, CONV_W), tok(COL_BG // CONV_W)),
            pl.BlockSpec((tm, CONV_W), tok(COL_CG // CONV_W)),
            pl.BlockSpec((tm, CONV_W), tok(COL_HV // CONV_W)),
            pl.BlockSpec((tm, GLA_DV), tok(COL_R // GLA_DV)),
            pl.BlockSpec((tm, D_MODEL), tok(COL_GA // D_MODEL)),
            pl.BlockSpec((tm, D_MODEL), tok(COL_GB // D_MODEL)),
            pl.BlockSpec((tm, GLA_DV), tok(0)),
            pl.BlockSpec((3, CONV_W), const2),
            pl.BlockSpec((1, CONV_W), const2),
            pl.BlockSpec((1, GLA_DV), const2),
            pl.BlockSpec((CONV_W, D_MODEL), const2),
            pl.BlockSpec((GLA_DV, D_MODEL), const2),
        ],
        out_specs=pl.BlockSpec((tm, D_MODEL), tok(0)),
        out_shape=jax.ShapeDtypeStruct((t, D_MODEL), BF16),
        compiler_params=pltpu.CompilerParams(
            dimension_semantics=("parallel",), vmem_limit_bytes=VMEM_LIMIT),
        name="mix",
    )(p, p, p, p, p, p, on, conv_w, conv_b, gng, wc, wg)
    return pl.pallas_call(
        functools.partial(_mix_out_kernel, row_chunk=tm // 2),
        grid=(t // tm,),
        in_specs=[
            pl.BlockSpec((tm, D_MODEL), tok(0)),
            pl.BlockSpec((tm, D_MODEL), tok(0)),
            pl.BlockSpec((1, N_MOD, D_MODEL), lambda i: (i // tiles_per_mod, 0, 0)),
            pl.BlockSpec((6, D_MODEL), const2),
            pl.BlockSpec((D_MODEL, D_MODEL), const2),
        ],
        out_specs=[pl.BlockSpec((tm, D_MODEL), tok(0)), pl.BlockSpec((tm, D_MODEL), tok(0))],
        out_shape=[jax.ShapeDtypeStruct((t, D_MODEL), F32), jax.ShapeDtypeStruct((t, D_MODEL), BF16)],
        compiler_params=pltpu.CompilerParams(
            dimension_semantics=("parallel",), vmem_limit_bytes=VMEM_LIMIT),
        name="mix_out",
    )(mix, x1, m3, norm_g, wo)


def kernel(x, c, ctx, c_ctx, w_mod, b_mod, norm_g, ffn1_w_in, ffn1_w_out, w_in, conv_w, conv_b,
           conv_out, gate_up, gate_bias, gla_norm_g, gla_out, w_o, ffn2_w_in, ffn2_w_out):
    assert w_mod.shape[0] == 1, "single-layer block"
    t = BATCH * SEQ
    tc = BATCH * CTX_LEN

    lo0 = PROJ_HEAD_BLOCKS * PROJ_TN
    w = w_in[0].astype(BF16)
    w_gates = w[:, lo0 + 2 * GATE_RANK:]
    w_lo = jnp.pad(w[:, lo0:lo0 + 2 * GATE_RANK], ((0, 0), (0, LO_PAD - 2 * GATE_RANK)))
    per_head = lambda a: a.reshape(GATE_RANK, GLA_HEADS, HEAD_K).transpose(1, 0, 2)
    gu = jnp.zeros((GLA_HEADS, LO_PAD, 2 * HEAD_K), F32)
    gu = gu.at[:, 0:GATE_RANK, :HEAD_K].set(per_head(gate_up[0, 0]))
    gu = gu.at[:, GATE_RANK:2 * GATE_RANK, HEAD_K:].set(per_head(gate_up[0, 1]))
    gu = gu.astype(BF16)
    gbias = jnp.concatenate([gate_bias[0, 0].reshape(GLA_HEADS, 1, HEAD_K),
                             gate_bias[0, 1].reshape(GLA_HEADS, 1, HEAD_K)], axis=2)
    g = norm_g[0]

    cc = jnp.concatenate([c, c_ctx[None, :], jnp.zeros((MOD_ROWS - BATCH - 1, D_MODEL), F32)], axis=0)
    m3 = _modulation(cc, w_mod[0], b_mod).reshape(MOD_ROWS, N_MOD, D_MODEL)

    xf = x.reshape(t, D_MODEL)
    cf = ctx.reshape(tc, D_MODEL)

    u1 = _prenorm(xf, m3, g, mod0=0, gidx=0, tm=1024, rows_per_mod=SEQ)
    act1, f1_out = _ffn_up(u1, ffn1_w_in, ffn1_w_out, tm=2048, tf=512)
    x1, u2, lo = _ffn_down(act1, xf, m3, g, f1_out, w_lo, mod0=0, gidx=0, next_mod0=3, next_gidx=2, tm=512,
                           rows_per_mod=SEQ)
    cu1 = _prenorm(cf, m3, g, mod0=0, gidx=0, tm=512, rows_per_mod=CTX_LEN, fixed_row=CTX_ROW)
    _, cu2, clo = _ffn_down(_ffn_up(cu1, ffn1_w_in, tm=1024, tf=512)[0], cf, m3, g, f1_out, w_lo, mod0=0, gidx=0,
                            next_mod0=3, next_gidx=2, tm=256, rows_per_mod=CTX_LEN, fixed_row=CTX_ROW)

    p, wc, wg, wo = _proj(u2, w, w_gates, (conv_out, gla_out, w_o), tm=2048, first_block=0,
                          head_blocks=PROJ_HEAD_BLOCKS, n_blocks=PROJ_BLOCKS, rot=PROJ_ROT)
    cp = _proj(cu2, w, w_gates, tm=1024, first_block=PROJ_K_BLOCK, head_blocks=3, n_blocks=3, rot=0)[0]

    on = _gla(p.reshape(BATCH, SEQ, P_COLS), lo.reshape(BATCH, SEQ, LO_PAD),
              cp.reshape(BATCH, CTX_LEN, GLA_DK + GLA_DV), clo.reshape(BATCH, CTX_LEN, LO_PAD),
              gu, gbias)

    x2, u3 = _merge(p, on.reshape(t, GLA_DV), x1, m3, g, conv_w[0], conv_b, gla_norm_g, wc, wg, wo, tm=512)
    act2, f2_out = _ffn_up(u3, ffn2_w_in, ffn2_w_out, tm=2048, tf=512)
    out = _ffn_down(act2, x2, m3, g, f2_out, None, mod0=6, gidx=4, next_mod0=None, next_gidx=None, tm=512,
                    rows_per_mod=SEQ)[0]
    return out.reshape(BATCH, SEQ, D_MODEL)
```

```python
import functools

import jax
import jax.numpy as jnp
from jax import lax
from jax.experimental import pallas as pl
from jax.experimental.pallas import tpu as pltpu

F32 = jnp.float32
BF16 = jnp.bfloat16

D_MODEL = 2048
BATCH = 4
SEQ = 4096
GRID_W = 64
CTX_LEN = 256
D_FF = 5632
MACARON_WEIGHT = 0.5
CONV_W = 1024
GLA_HEADS = 4
GLA_DK = 1024
GLA_DV = 2048
HEAD_K = GLA_DK // GLA_HEADS
HEAD_V = GLA_DV // GLA_HEADS
GATE_RANK = 16
GATE_TAU = 16.0
N_MOD = 9
EPS = 1e-6
LOG2_E = 1.4426950408889634
Q_SCALE_LOG2 = 4.0
assert 2.0 ** (2 * Q_SCALE_LOG2) == HEAD_K

LANES = 128
MOD_ROWS = 8
CTX_ROW = BATCH
GLA_CHUNK = 64
GLA_PREP_CHUNKS = 8
FFN_UP_ROW_CHUNK = 256

PRENORM_TM, CTX_PRENORM_TM = 1024, 512
FFN_UP_TM, CTX_FFN_UP_TM, FFN_TF = 2048, 1024, 512
FFN_DOWN_TM, CTX_FFN_DOWN_TM = 512, 256
PROJ_TM, CTX_PROJ_TM = 2048, 1024
MERGE_TM = 512
MOD_TN = 1024
LO_PAD = LANES

PROJ_TN = 1024
PROJ_HEAD_BLOCKS = 9
PROJ_BLOCKS = 13
PROJ_ROT = 8
PROJ_K_BLOCK = 4
PROJ_SIDE_SLABS = 64
COL_V, COL_R, COL_GA, COL_GB = 0, 2048, 4096, 6144
COL_BG, COL_CG, COL_HV, COL_Q, COL_K = 8192, 9216, 10240, 11264, 12288
P_COLS = PROJ_BLOCKS * PROJ_TN

VMEM_LIMIT = 56 * 1024 * 1024
BIG_VMEM_LIMIT = 60 * 1024 * 1024


def _dot(a, b):
    return jnp.dot(a, b, preferred_element_type=F32)


def _dot_t_lhs(a, b):
    return lax.dot_general(a, b, (((0,), (0,)), ((), ())), preferred_element_type=F32)


def _dot_t_rhs(a, b):
    return lax.dot_general(a, b, (((1,), (1,)), ((), ())), preferred_element_type=F32)


def _rms(x):
    return x * lax.rsqrt(jnp.mean(x * x, axis=-1, keepdims=True) + EPS)


def _silu(x):
    return x * jax.nn.sigmoid(x)


def _mod_kernel(c_ref, w_ref, b_ref, o_ref):
    s = _silu(c_ref[...]).astype(BF16)
    o_ref[...] = _dot(s, w_ref[...].astype(BF16)) + b_ref[...]


def _modulation(cc, w_mod, b_mod):
    n = w_mod.shape[1]
    tn = MOD_TN
    return pl.pallas_call(
        _mod_kernel,
        grid=(n // tn,),
        in_specs=[
            pl.BlockSpec((MOD_ROWS, D_MODEL), lambda j: (0, 0)),
            pl.BlockSpec((D_MODEL, tn), lambda j: (0, j)),
            pl.BlockSpec((1, tn), lambda j: (0, j)),
        ],
        out_specs=pl.BlockSpec((MOD_ROWS, tn), lambda j: (0, j)),
        out_shape=jax.ShapeDtypeStruct((MOD_ROWS, n), F32),
        compiler_params=pltpu.CompilerParams(
            dimension_semantics=("arbitrary",), vmem_limit_bytes=VMEM_LIMIT),
        name="modulation",
    )(cc, w_mod, b_mod)


def _mod_norm(h, m_ref, g_ref, mod0, gidx):
    shift = m_ref[0, mod0:mod0 + 1, :]
    gain = g_ref[gidx:gidx + 1, :] * (1.0 + m_ref[0, mod0 + 1:mod0 + 2, :])
    return (_rms(h) * gain + shift).astype(BF16)


def _mod_row_map(tm, rows_per_mod, fixed_row):
    if fixed_row is not None:
        return lambda i, *_: (fixed_row, 0, 0)
    tiles_per_mod = rows_per_mod // tm
    return lambda i, *_: (i // tiles_per_mod, 0, 0)


def _prenorm_kernel(h_ref, m_ref, g_ref, u_ref, *, mod0, gidx):
    u_ref[...] = _mod_norm(h_ref[...], m_ref, g_ref, mod0, gidx)


def _prenorm(h, m3, norm_g, *, mod0, gidx, tm, rows_per_mod, fixed_row=None):
    t = h.shape[0]
    return pl.pallas_call(
        functools.partial(_prenorm_kernel, mod0=mod0, gidx=gidx),
        grid=(t // tm,),
        in_specs=[
            pl.BlockSpec((tm, D_MODEL), lambda i: (i, 0)),
            pl.BlockSpec((1, N_MOD, D_MODEL), _mod_row_map(tm, rows_per_mod, fixed_row)),
            pl.BlockSpec((6, D_MODEL), lambda i: (0, 0)),
        ],
        out_specs=pl.BlockSpec((tm, D_MODEL), lambda i: (i, 0)),
        out_shape=jax.ShapeDtypeStruct((t, D_MODEL), BF16),
        compiler_params=pltpu.CompilerParams(
            dimension_semantics=("parallel",), vmem_limit_bytes=VMEM_LIMIT),
        name="prenorm",
    )(h, m3, norm_g)


def _ffn_up_kernel(*refs, cast_w_out):
    if cast_w_out:
        u_ref, wa_ref, wb_ref, wo_ref, act_ref, wo_bf_ref, wa_bf, wb_bf = refs
        wo_bf_ref[...] = wo_ref[0].astype(BF16)
    else:
        u_ref, wa_ref, wb_ref, act_ref, wa_bf, wb_bf = refs

    @pl.when(pl.program_id(1) == 0)
    def _():
        wa_bf[...] = wa_ref[0].astype(BF16)
        wb_bf[...] = wb_ref[0].astype(BF16)

    for r in range(0, u_ref.shape[0], FFN_UP_ROW_CHUNK):
        u = u_ref[r:r + FFN_UP_ROW_CHUNK, :]
        a = _dot(u, wa_bf[...])
        b = _dot(u, wb_bf[...])
        act_ref[r:r + FFN_UP_ROW_CHUNK, :] = (_silu(b) * a).astype(BF16)


def _ffn_up(u, w_in3, w_out3=None, *, tm, tf):
    t = u.shape[0]
    nf = D_FF // tf
    n_i = t // tm
    in_specs = [
        pl.BlockSpec((tm, D_MODEL), lambda j, i: (i, 0)),
        pl.BlockSpec((1, D_MODEL, tf), lambda j, i: (0, 0, j)),
        pl.BlockSpec((1, D_MODEL, tf), lambda j, i: (0, 0, j + nf)),
    ]
    out_specs = [pl.BlockSpec((tm, tf), lambda j, i: (i, j))]
    out_shape = [jax.ShapeDtypeStruct((t, D_FF), BF16)]
    args = [u, w_in3, w_in3]
    if w_out3 is not None:
        slab = D_FF // (nf * n_i)
        assert slab * nf * n_i == D_FF and slab % 16 == 0, "one bf16-tileable row slab of w_out per grid step"
        in_specs.append(pl.BlockSpec((1, slab, D_MODEL), lambda j, i: (0, j * n_i + i, 0)))
        out_specs.append(pl.BlockSpec((slab, D_MODEL), lambda j, i: (j * n_i + i, 0)))
        out_shape.append(jax.ShapeDtypeStruct((D_FF, D_MODEL), BF16))
        args.append(w_out3)
    return pl.pallas_call(
        functools.partial(_ffn_up_kernel, cast_w_out=w_out3 is not None),
        grid=(nf, n_i),
        in_specs=in_specs,
        out_specs=out_specs,
        out_shape=out_shape,
        scratch_shapes=[pltpu.VMEM((D_MODEL, tf), BF16), pltpu.VMEM((D_MODEL, tf), BF16)],
        compiler_params=pltpu.CompilerParams(
            dimension_semantics=("parallel", "arbitrary"), vmem_limit_bytes=VMEM_LIMIT),
        name="ffn_up",
    )(*args)


def _ffn_down_kernel(*refs, mod0, gidx, next_mod0, next_gidx, row_chunk):
    if next_mod0 is None:
        act_ref, h_ref, m_ref, g_ref, wo_ref, o_ref = refs
    else:
        act_ref, h_ref, m_ref, g_ref, wo_ref, wlo_ref, o_ref, u_ref, lo_ref = refs
    gain = (MACARON_WEIGHT * m_ref[0, mod0 + 2:mod0 + 3, :]) * g_ref[gidx + 1:gidx + 2, :]
    us = []
    for r in range(0, h_ref.shape[0], row_chunk):
        rows = slice(r, r + row_chunk)
        y = _dot(act_ref[rows, :], wo_ref[...])
        o = h_ref[rows, :] + _rms(y) * gain
        o_ref[rows, :] = o
        if next_mod0 is not None:
            us.append(_mod_norm(o, m_ref, g_ref, next_mod0, next_gidx))
            u_ref[rows, :] = us[-1]
    if next_mod0 is not None:
        lo_ref[...] = _dot(jnp.concatenate(us, axis=0), wlo_ref[...]).astype(BF16)


def _ffn_down(act, h, m3, norm_g, w_out, w_lo, *, mod0, gidx, next_mod0, next_gidx, tm, rows_per_mod,
              fixed_row=None):
    t = h.shape[0]
    emit_next = next_mod0 is not None
    tok = lambda i: (i, 0)
    const = lambda i: (0, 0)
    in_specs = [
        pl.BlockSpec((tm, D_FF), tok),
        pl.BlockSpec((tm, D_MODEL), tok),
        pl.BlockSpec((1, N_MOD, D_MODEL), _mod_row_map(tm, rows_per_mod, fixed_row)),
        pl.BlockSpec((6, D_MODEL), const),
        pl.BlockSpec((D_FF, D_MODEL), const),
    ]
    out_specs = [pl.BlockSpec((tm, D_MODEL), tok)]
    out_shape = [jax.ShapeDtypeStruct((t, D_MODEL), F32)]
    args = [act, h, m3, norm_g, w_out]
    if emit_next:
        in_specs.append(pl.BlockSpec((D_MODEL, LO_PAD), const))
        out_specs += [pl.BlockSpec((tm, D_MODEL), tok), pl.BlockSpec((tm, LO_PAD), tok)]
        out_shape += [jax.ShapeDtypeStruct((t, D_MODEL), BF16), jax.ShapeDtypeStruct((t, LO_PAD), BF16)]
        args.append(w_lo)
    return pl.pallas_call(
        functools.partial(_ffn_down_kernel, mod0=mod0, gidx=gidx, next_mod0=next_mod0, next_gidx=next_gidx,
                          row_chunk=tm // 2),
        grid=(t // tm,),
        in_specs=in_specs,
        out_specs=out_specs,
        out_shape=out_shape,
        compiler_params=pltpu.CompilerParams(
            dimension_semantics=("parallel",), vmem_limit_bytes=BIG_VMEM_LIMIT),
        name="ffn_down",
    )(*args)


def _proj_kernel(*refs, head_blocks, n_side):
    u_ref, w_ref, wg_ref = refs[:3]
    side_in = refs[3:3 + n_side]
    p_ref = refs[3 + n_side]
    side_out = refs[4 + n_side:]
    n = pl.program_id(0)

    for src, dst in zip(side_in, side_out):
        dst[...] = src[0].astype(BF16)

    wsel = jnp.where(n < head_blocks, w_ref[...], wg_ref[...])
    p_ref[...] = _dot(u_ref[...], wsel).astype(BF16)


def _proj(u, w, w_gates, side=(), *, tm, first_block, head_blocks, n_blocks, rot):
    t = u.shape[0]
    tn = PROJ_TN
    n_i = t // tm
    assert not side or n_blocks * n_i >= PROJ_SIDE_SLABS
    slab_idx = lambda n, i: jnp.minimum(n * n_i + i, PROJ_SIDE_SLABS - 1)
    in_specs = [
        pl.BlockSpec((tm, D_MODEL), lambda n, i: (i, 0)),
        pl.BlockSpec((D_MODEL, tn), lambda n, i: (0, first_block + jnp.minimum(n, head_blocks - 1))),
        pl.BlockSpec((D_MODEL, tn), lambda n, i: (0, jnp.maximum(n - head_blocks, 0))),
    ]
    out_specs = [pl.BlockSpec((tm, tn), lambda n, i: (i, (n + rot) % n_blocks))]
    out_shape = [jax.ShapeDtypeStruct((t, n_blocks * tn), BF16)]
    for a in side:
        rows, cols = a.shape[1] // PROJ_SIDE_SLABS, a.shape[2]
        assert rows * PROJ_SIDE_SLABS == a.shape[1] and rows % 16 == 0, "bf16-tileable row slabs"
        in_specs.append(pl.BlockSpec((1, rows, cols), lambda n, i: (0, slab_idx(n, i), 0)))
        out_specs.append(pl.BlockSpec((rows, cols), lambda n, i: (slab_idx(n, i), 0)))
        out_shape.append(jax.ShapeDtypeStruct(a.shape[1:], BF16))
    return pl.pallas_call(
        functools.partial(_proj_kernel, head_blocks=head_blocks, n_side=len(side)),
        grid=(n_blocks, n_i),
        in_specs=in_specs,
        out_specs=out_specs,
        out_shape=out_shape,
        compiler_params=pltpu.CompilerParams(
            dimension_semantics=("arbitrary", "arbitrary"), vmem_limit_bytes=VMEM_LIMIT),
        name="proj",
    )(u, w, w_gates, *side)


def _log2_decay(lo_bf16, gu, bias):
    z = _dot(lo_bf16, gu) + bias
    return (jnp.minimum(z, 0.0) - jnp.log(1.0 + jnp.exp(-jnp.abs(z)))) * (LOG2_E / GATE_TAU)


def _split2(x):
    hi = x.astype(BF16)
    lo = (x - hi.astype(F32)).astype(BF16)
    return jnp.concatenate([hi, lo], axis=0)


def _tri2(n, upper, strict):
    r = lax.broadcasted_iota(jnp.int32, (n, 2 * n), 0)
    c = lax.broadcasted_iota(jnp.int32, (n, 2 * n), 1) & (n - 1)
    if upper:
        m = (c > r) if strict else (c >= r)
    else:
        m = (c < r) if strict else (c <= r)
    return m.astype(BF16)


def _gla_kernel(q_ref, k_ref, v_ref, lo_ref, ck_ref, cv_ref, clo_ref, gu_ref, gb_ref, on_ref,
                qd_f, kd_f, qd_b, kd_b, o_acc, o_bwd, a_f, a_b, st_f, st_b):
    C = GLA_CHUNK
    n_chunks = SEQ // C
    n_pairs = n_chunks // 2
    gu = gu_ref[0]
    bias = gb_ref[0]

    la_c = _log2_decay(clo_ref[0], gu, bias)
    ck = ck_ref[0].astype(F32)
    cv = cv_ref[0]
    dec_f = jnp.exp2(_dot(_tri2(CTX_LEN, True, True), _split2(la_c[:, :HEAD_K])))
    dec_b = jnp.exp2(_dot(_tri2(CTX_LEN, False, True), _split2(la_c[:, HEAD_K:])))
    st_f[...] = _dot_t_lhs(cv, (ck * dec_f).astype(BF16))
    st_b[...] = _dot_t_lhs(cv, (ck * dec_b).astype(BF16))

    tr = lax.broadcasted_iota(jnp.int32, (2 * C, 2 * C), 0)
    tc = lax.broadcasted_iota(jnp.int32, (2 * C, 2 * C), 1) & (C - 1)
    tri_fb = (((tr < C) & (tc <= tr)) | ((tr >= C) & (tc >= tr - C))).astype(BF16)
    pr = lax.broadcasted_iota(jnp.int32, (C, 2 * C), 0)
    pc = lax.broadcasted_iota(jnp.int32, (C, 2 * C), 1)
    keep_f = pc <= pr
    keep_b = pc - C >= pr

    def prepare(i, carry):
        r0 = pl.multiple_of(i * (GLA_PREP_CHUNKS * C), GLA_PREP_CHUNKS * C)
        la = _log2_decay(lo_ref[0, pl.ds(r0, GLA_PREP_CHUNKS * C), :], gu, bias)
        cum = [_dot(tri_fb, _split2(la[j * C:(j + 1) * C, :])) for j in range(GLA_PREP_CHUNKS)]
        chunks = []
        for j in range(GLA_PREP_CHUNKS):
            rows = pl.ds(r0 + j * C, C)
            qc = q_ref[0, rows, :].astype(F32)
            kc = k_ref[0, rows, :].astype(F32)
            b_f = cum[j][:C, :HEAD_K]
            b_b = cum[j][C:, HEAD_K:]
            dirs = []
            for fwd, b in ((True, b_f), (False, b_b)):
                if fwd:
                    b_mid, b_last = b[C // 2 - 1:C // 2, :], b[C - 1:C, :]
                else:
                    b_mid, b_last = b[C // 2:C // 2 + 1, :], b[0:1, :]
                qs = qc * jnp.exp2(b - (b_mid + Q_SCALE_LOG2))
                ks = kc * jnp.exp2(b_mid - b)
                dirs.append(dict(qs=qs.astype(BF16), ks=ks.astype(BF16), qd=qs * jnp.exp2(b_mid),
                                 kd=ks * jnp.exp2(b_last - b_mid), a=jnp.exp2(b_last)))
            scores = _dot_t_rhs(jnp.concatenate([dirs[0]["qs"], dirs[1]["qs"]], axis=0),
                                jnp.concatenate([dirs[0]["ks"], dirs[1]["ks"]], axis=0))
            chunks.append(dict(rows=rows, f=dirs[0], b=dirs[1], scores=scores))
        crosses = []
        for p in range(GLA_PREP_CHUNKS // 2):
            lo_c, hi_c = chunks[2 * p], chunks[2 * p + 1]
            lo_f, hi_f, lo_b, hi_b = lo_c["f"], hi_c["f"], lo_c["b"], hi_c["b"]
            qd_f[lo_c["rows"], :] = lo_f["qd"].astype(BF16)
            qd_f[hi_c["rows"], :] = (hi_f["qd"] * lo_f["a"]).astype(BF16)
            kd_f[lo_c["rows"], :] = (lo_f["kd"] * hi_f["a"]).astype(BF16)
            kd_f[hi_c["rows"], :] = hi_f["kd"].astype(BF16)
            qd_b[hi_c["rows"], :] = hi_b["qd"].astype(BF16)
            qd_b[lo_c["rows"], :] = (lo_b["qd"] * hi_b["a"]).astype(BF16)
            kd_b[hi_c["rows"], :] = (hi_b["kd"] * lo_b["a"]).astype(BF16)
            kd_b[lo_c["rows"], :] = lo_b["kd"].astype(BF16)
            pair = pl.ds(i * (GLA_PREP_CHUNKS // 2) + p, 1)
            a_f[pair, :] = lo_f["a"] * hi_f["a"]
            a_b[pair, :] = lo_b["a"] * hi_b["a"]
            crosses.append(_dot_t_rhs(
                jnp.concatenate([hi_f["qd"].astype(BF16), lo_b["qd"].astype(BF16)], axis=0),
                jnp.concatenate([lo_f["kd"].astype(BF16), hi_b["kd"].astype(BF16)], axis=0)))
        for p in range(GLA_PREP_CHUNKS // 2):
            lo_c, hi_c = chunks[2 * p], chunks[2 * p + 1]
            v_lo = v_ref[0, lo_c["rows"], :]
            v_hi = v_ref[0, hi_c["rows"], :]
            zero = jnp.zeros((C, 2 * C), F32)
            w_lo = jnp.concatenate(
                [jnp.where(keep_f, lo_c["scores"][:C, :], jnp.where(keep_b, lo_c["scores"][C:, :], 0.0)),
                 jnp.where(pc >= C, crosses[p][C:, :], zero)], axis=1)
            o_acc[lo_c["rows"], :] = _dot(w_lo.astype(BF16), jnp.concatenate([v_lo, v_lo, v_lo, v_hi], axis=0))
            w_hi = jnp.concatenate(
                [jnp.where(keep_f, hi_c["scores"][:C, :], jnp.where(keep_b, hi_c["scores"][C:, :], 0.0)),
                 jnp.where(pc < C, crosses[p][:C, :], zero)], axis=1)
            o_acc[hi_c["rows"], :] = _dot(w_hi.astype(BF16), jnp.concatenate([v_hi, v_hi, v_lo, v_lo], axis=0))
        return carry

    def scan(i, carry, final):
        steps = ((i, qd_f, kd_f, a_f, st_f), (n_pairs - 1 - i, qd_b, kd_b, a_b, st_b))
        rows = [pl.ds(pl.multiple_of(c * (2 * C), 2 * C), 2 * C) for c, *_ in steps]
        incs = [_dot_t_lhs(v_ref[0, r, :], kd[r, :]) for r, (_, _, kd, _, _) in zip(rows, steps)]
        outs = [_dot_t_rhs(qd[r, :], st[...].astype(BF16)) for r, (_, qd, _, _, st) in zip(rows, steps)]
        if not final:
            o_acc[rows[0], :] += outs[0]
            upper = pl.ds(pl.multiple_of((n_pairs // 2 - 1 - i) * (2 * C), 2 * C), 2 * C)
            o_bwd[upper, :] = outs[1]
        else:
            upper = pl.ds(pl.multiple_of((i - n_pairs // 2) * (2 * C), 2 * C), 2 * C)
            on_ref[0, rows[0], :] = _rms(o_acc[rows[0], :] + o_bwd[upper, :] + outs[0]).astype(BF16)
            on_ref[0, rows[1], :] = _rms(o_acc[rows[1], :] + outs[1]).astype(BF16)
        for inc, (c, _, _, a, st) in zip(incs, steps):
            st[...] = st[...] * a[pl.ds(c, 1), :] + inc
        return carry

    lax.fori_loop(0, n_chunks // GLA_PREP_CHUNKS, prepare, 0)
    lax.fori_loop(0, n_pairs // 2, functools.partial(scan, final=False), 0, unroll=4)
    lax.fori_loop(n_pairs // 2, n_pairs, functools.partial(scan, final=True), 0, unroll=4)


def _gla(p3, lo3, cp3, clo3, gu, gbias):
    qb, kb, vb = COL_Q // HEAD_K, COL_K // HEAD_K, COL_V // HEAD_V
    n_chunks = SEQ // GLA_CHUNK
    return pl.pallas_call(
        _gla_kernel,
        grid=(BATCH, GLA_HEADS),
        in_specs=[
            pl.BlockSpec((1, SEQ, HEAD_K), lambda b, h: (b, 0, qb + h)),
            pl.BlockSpec((1, SEQ, HEAD_K), lambda b, h: (b, 0, kb + h)),
            pl.BlockSpec((1, SEQ, HEAD_V), lambda b, h: (b, 0, vb + h)),
            pl.BlockSpec((1, SEQ, LO_PAD), lambda b, h: (b, 0, 0)),
            pl.BlockSpec((1, CTX_LEN, HEAD_K), lambda b, h: (b, 0, h)),
            pl.BlockSpec((1, CTX_LEN, HEAD_V), lambda b, h: (b, 0, GLA_DK // HEAD_V + h)),
            pl.BlockSpec((1, CTX_LEN, LO_PAD), lambda b, h: (b, 0, 0)),
            pl.BlockSpec((1, LO_PAD, 2 * HEAD_K), lambda b, h: (h, 0, 0)),
            pl.BlockSpec((1, 1, 2 * HEAD_K), lambda b, h: (h, 0, 0)),
        ],
        out_specs=pl.BlockSpec((1, SEQ, HEAD_V), lambda b, h: (b, 0, h)),
        out_shape=jax.ShapeDtypeStruct((BATCH, SEQ, GLA_DV), BF16),
        scratch_shapes=[
            pltpu.VMEM((SEQ, HEAD_K), BF16),
            pltpu.VMEM((SEQ, HEAD_K), BF16),
            pltpu.VMEM((SEQ, HEAD_K), BF16),
            pltpu.VMEM((SEQ, HEAD_K), BF16),
            pltpu.VMEM((SEQ, HEAD_V), F32),
            pltpu.VMEM((SEQ // 2, HEAD_V), F32),
            pltpu.VMEM((n_chunks // 2, HEAD_K), F32),
            pltpu.VMEM((n_chunks // 2, HEAD_K), F32),
            pltpu.VMEM((HEAD_V, HEAD_K), F32),
            pltpu.VMEM((HEAD_V, HEAD_K), F32),
        ],
        compiler_params=pltpu.CompilerParams(
            dimension_semantics=("parallel", "arbitrary"), vmem_limit_bytes=BIG_VMEM_LIMIT),
        name="gla",
    )(p3, p3, p3, lo3, cp3, cp3, clo3, gu, gbias)


def _mix_kernel(bg_ref, cg_ref, hv_ref, r_ref, ga_ref, gb_ref, on_ref, cw_ref, cb_ref, gng_ref, wc_ref, wg_ref,
                mix_ref, *, row_chunk):
    chunks = [slice(r, r + row_chunk) for r in range(0, mix_ref.shape[0], row_chunk)]
    col = lax.broadcasted_iota(jnp.int32, (row_chunk, 1), 0) & (GRID_W - 1)
    branches = []
    for rows in chunks:
        z = cg_ref[rows, :].astype(F32) * hv_ref[rows, :].astype(F32)
        z_prev = jnp.where(col == 0, 0.0, pltpu.roll(z, 1, 0))
        z_next = jnp.where(col == GRID_W - 1, 0.0, pltpu.roll(z, row_chunk - 1, 0))
        y = z_prev * cw_ref[0:1, :] + z * cw_ref[1:2, :] + z_next * cw_ref[2:3, :] + cb_ref[...]
        y_a = _dot((bg_ref[rows, :].astype(F32) * y).astype(BF16), wc_ref[...])
        rd = on_ref[rows, :].astype(F32) * gng_ref[...] * _silu(r_ref[rows, :].astype(F32))
        branches.append((y_a, _dot(rd.astype(BF16), wg_ref[...])))
    for rows, (y_a, y_b) in zip(chunks, branches):
        mix_ref[rows, :] = (jax.nn.sigmoid(ga_ref[rows, :].astype(F32)) * y_a
                            + jax.nn.sigmoid(gb_ref[rows, :].astype(F32)) * y_b).astype(BF16)


def _mix_out_kernel(mix_ref, x_ref, m_ref, g_ref, wo_ref, o_ref, u_ref, *, row_chunk):
    gain = m_ref[0, 5:6, :] * g_ref[3:4, :]
    for r in range(0, x_ref.shape[0], row_chunk):
        rows = slice(r, r + row_chunk)
        o = x_ref[rows, :] + _rms(_dot(mix_ref[rows, :], wo_ref[...])) * gain
        o_ref[rows, :] = o
        u_ref[rows, :] = _mod_norm(o, m_ref, g_ref, 6, 4)


def _merge(p, on, x1, m3, norm_g, conv_w, conv_b, gng, wc, wg, wo, *, tm):
    t = x1.shape[0]
    tiles_per_mod = SEQ // tm
    tok = lambda blk: (lambda i: (i, blk))
    const2 = lambda i: (0, 0)
    mix = pl.pallas_call(
        functools.partial(_mix_kernel, row_chunk=tm // 2),
        grid=(t // tm,),
        in_specs=[
            pl.BlockSpec((tm, CONV_W), tok(COL_BG // CONV_W)),
            pl.BlockSpec((tm, CONV_W), tok(COL_CG // CONV_W)),
            pl.BlockSpec((tm, CONV_W), tok(COL_HV // CONV_W)),
            pl.BlockSpec((tm, GLA_DV), tok(COL_R // GLA_DV)),
            pl.BlockSpec((tm, D_MODEL), tok(COL_GA // D_MODEL)),
            pl.BlockSpec((tm, D_MODEL), tok(COL_GB // D_MODEL)),
            pl.BlockSpec((tm, GLA_DV), tok(0)),
            pl.BlockSpec((3, CONV_W), const2),
            pl.BlockSpec((1, CONV_W), const2),
            pl.BlockSpec((1, GLA_DV), const2),
            pl.BlockSpec((CONV_W, D_MODEL), const2),
            pl.BlockSpec((GLA_DV, D_MODEL), const2),
        ],
        out_specs=pl.BlockSpec((tm, D_MODEL), tok(0)),
        out_shape=jax.ShapeDtypeStruct((t, D_MODEL), BF16),
        compiler_params=pltpu.CompilerParams(
            dimension_semantics=("parallel",), vmem_limit_bytes=VMEM_LIMIT),
        name="mix",
    )(p, p, p, p, p, p, on, conv_w, conv_b, gng, wc, wg)
    return pl.pallas_call(
        functools.partial(_mix_out_kernel, row_chunk=tm // 2),
        grid=(t // tm,),
        in_specs=[
            pl.BlockSpec((tm, D_MODEL), tok(0)),
            pl.BlockSpec((tm, D_MODEL), tok(0)),
            pl.BlockSpec((1, N_MOD, D_MODEL), lambda i: (i // tiles_per_mod, 0, 0)),
            pl.BlockSpec((6, D_MODEL), const2),
            pl.BlockSpec((D_MODEL, D_MODEL), const2),
        ],
        out_specs=[pl.BlockSpec((tm, D_MODEL), tok(0)), pl.BlockSpec((tm, D_MODEL), tok(0))],
        out_shape=[jax.ShapeDtypeStruct((t, D_MODEL), F32), jax.ShapeDtypeStruct((t, D_MODEL), BF16)],
        compiler_params=pltpu.CompilerParams(
            dimension_semantics=("parallel",), vmem_limit_bytes=VMEM_LIMIT),
        name="mix_out",
    )(mix, x1, m3, norm_g, wo)


def kernel(x, c, ctx, c_ctx, w_mod, b_mod, norm_g, ffn1_w_in, ffn1_w_out, w_in, conv_w, conv_b,
           conv_out, gate_up, gate_bias, gla_norm_g, gla_out, w_o, ffn2_w_in, ffn2_w_out):
    assert w_mod.shape[0] == 1, "single-layer block"
    t = BATCH * SEQ
    tc = BATCH * CTX_LEN

    lo0 = PROJ_HEAD_BLOCKS * PROJ_TN
    w = w_in[0].astype(BF16)
    w_gates = w[:, lo0 + 2 * GATE_RANK:]
    w_lo = jnp.pad(w[:, lo0:lo0 + 2 * GATE_RANK], ((0, 0), (0, LO_PAD - 2 * GATE_RANK)))
    per_head = lambda a: a.reshape(GATE_RANK, GLA_HEADS, HEAD_K).transpose(1, 0, 2)
    gu = jnp.zeros((GLA_HEADS, LO_PAD, 2 * HEAD_K), F32)
    gu = gu.at[:, 0:GATE_RANK, :HEAD_K].set(per_head(gate_up[0, 0]))
    gu = gu.at[:, GATE_RANK:2 * GATE_RANK, HEAD_K:].set(per_head(gate_up[0, 1]))
    gu = gu.astype(BF16)
    gbias = jnp.concatenate([gate_bias[0, 0].reshape(GLA_HEADS, 1, HEAD_K),
                             gate_bias[0, 1].reshape(GLA_HEADS, 1, HEAD_K)], axis=2)
    g = norm_g[0]

    cc = jnp.concatenate([c, c_ctx[None, :], jnp.zeros((MOD_ROWS - BATCH - 1, D_MODEL), F32)], axis=0)
    m3 = _modulation(cc, w_mod[0], b_mod).reshape(MOD_ROWS, N_MOD, D_MODEL)

    xf = x.reshape(t, D_MODEL)
    cf = ctx.reshape(tc, D_MODEL)

    u1 = _prenorm(xf, m3, g, mod0=0, gidx=0, tm=PRENORM_TM, rows_per_mod=SEQ)
    act1, f1_out = _ffn_up(u1, ffn1_w_in, ffn1_w_out, tm=FFN_UP_TM, tf=FFN_TF)
    x1, u2, lo = _ffn_down(act1, xf, m3, g, f1_out, w_lo, mod0=0, gidx=0, next_mod0=3, next_gidx=2,
                           tm=FFN_DOWN_TM, rows_per_mod=SEQ)
    cu1 = _prenorm(cf, m3, g, mod0=0, gidx=0, tm=CTX_PRENORM_TM, rows_per_mod=CTX_LEN, fixed_row=CTX_ROW)
    cact = _ffn_up(cu1, ffn1_w_in, tm=CTX_FFN_UP_TM, tf=FFN_TF)[0]
    _, cu2, clo = _ffn_down(cact, cf, m3, g, f1_out, w_lo, mod0=0, gidx=0, next_mod0=3, next_gidx=2,
                            tm=CTX_FFN_DOWN_TM, rows_per_mod=CTX_LEN, fixed_row=CTX_ROW)

    p, wc, wg, wo = _proj(u2, w, w_gates, (conv_out, gla_out, w_o), tm=PROJ_TM, first_block=0,
                          head_blocks=PROJ_HEAD_BLOCKS, n_blocks=PROJ_BLOCKS, rot=PROJ_ROT)
    cp = _proj(cu2, w, w_gates, tm=CTX_PROJ_TM, first_block=PROJ_K_BLOCK, head_blocks=3, n_blocks=3, rot=0)[0]

    on = _gla(p.reshape(BATCH, SEQ, P_COLS), lo.reshape(BATCH, SEQ, LO_PAD),
              cp.reshape(BATCH, CTX_LEN, GLA_DK + GLA_DV), clo.reshape(BATCH, CTX_LEN, LO_PAD),
              gu, gbias)

    x2, u3 = _merge(p, on.reshape(t, GLA_DV), x1, m3, g, conv_w[0], conv_b, gla_norm_g, wc, wg, wo, tm=MERGE_TM)
    act2, f2_out = _ffn_up(u3, ffn2_w_in, ffn2_w_out, tm=FFN_UP_TM, tf=FFN_TF)
    out = _ffn_down(act2, x2, m3, g, f2_out, None, mod0=6, gidx=4, next_mod0=None, next_gidx=None,
                    tm=FFN_DOWN_TM, rows_per_mod=SEQ)[0]
    return out.reshape(BATCH, SEQ, D_MODEL)
```

```python
import functools

import jax
import jax.numpy as jnp
from jax import lax
from jax.experimental import pallas as pl
from jax.experimental.pallas import tpu as pltpu

F32 = jnp.float32
BF16 = jnp.bfloat16

D_MODEL = 2048
BATCH = 4
SEQ = 4096
GRID_W = 64
CTX_LEN = 256
D_FF = 5632
MACARON_WEIGHT = 0.5
CONV_W = 1024
GLA_HEADS = 4
GLA_DK = 1024
GLA_DV = 2048
HEAD_K = GLA_DK // GLA_HEADS
HEAD_V = GLA_DV // GLA_HEADS
GATE_RANK = 16
GATE_TAU = 16.0
N_MOD = 9
EPS = 1e-6
LOG2_E = 1.4426950408889634
Q_SCALE_LOG2 = 4.0
assert 2.0 ** (2 * Q_SCALE_LOG2) == HEAD_K

LANES = 128
MOD_ROWS = 8
CTX_ROW = BATCH
GLA_CHUNK = 64
GLA_PREP_CHUNKS = 8
FFN_UP_ROW_CHUNK = 256
FFN_DOWN_W_SLABS = 11

PRENORM_TM, CTX_PRENORM_TM = 1024, 512
FFN_UP_TM, CTX_FFN_UP_TM, FFN_TF = 2048, 1024, 512
FFN_DOWN_TM, CTX_FFN_DOWN_TM = 512, 256
PROJ_TM, CTX_PROJ_TM = 2048, 1024
MERGE_TM = 512
MOD_TN = 1024
LO_PAD = LANES

PROJ_TN = 1024
PROJ_HEAD_BLOCKS = 9
PROJ_BLOCKS = 13
PROJ_ROT = 8
PROJ_K_BLOCK = 4
PROJ_SIDE_SLABS = 64
COL_V, COL_R, COL_GA, COL_GB = 0, 2048, 4096, 6144
COL_BG, COL_CG, COL_HV, COL_Q, COL_K = 8192, 9216, 10240, 11264, 12288
P_COLS = PROJ_BLOCKS * PROJ_TN

VMEM_LIMIT = 56 * 1024 * 1024
BIG_VMEM_LIMIT = 60 * 1024 * 1024


def _dot(a, b):
    return jnp.dot(a, b, preferred_element_type=F32)


def _dot_t_lhs(a, b):
    return lax.dot_general(a, b, (((0,), (0,)), ((), ())), preferred_element_type=F32)


def _dot_t_rhs(a, b):
    return lax.dot_general(a, b, (((1,), (1,)), ((), ())), preferred_element_type=F32)


def _rms(x):
    return x * lax.rsqrt(jnp.mean(x * x, axis=-1, keepdims=True) + EPS)


def _silu(x):
    return x * jax.nn.sigmoid(x)


def _mod_kernel(c_ref, w_ref, b_ref, o_ref):
    s = _silu(c_ref[...]).astype(BF16)
    o_ref[...] = _dot(s, w_ref[...].astype(BF16)) + b_ref[...]


def _modulation(cc, w_mod, b_mod):
    n = w_mod.shape[1]
    tn = MOD_TN
    return pl.pallas_call(
        _mod_kernel,
        grid=(n // tn,),
        in_specs=[
            pl.BlockSpec((MOD_ROWS, D_MODEL), lambda j: (0, 0)),
            pl.BlockSpec((D_MODEL, tn), lambda j: (0, j)),
            pl.BlockSpec((1, tn), lambda j: (0, j)),
        ],
        out_specs=pl.BlockSpec((MOD_ROWS, tn), lambda j: (0, j)),
        out_shape=jax.ShapeDtypeStruct((MOD_ROWS, n), F32),
        compiler_params=pltpu.CompilerParams(
            dimension_semantics=("arbitrary",), vmem_limit_bytes=VMEM_LIMIT),
        name="modulation",
    )(cc, w_mod, b_mod)


def _mod_norm(h, m_ref, g_ref, mod0, gidx):
    shift = m_ref[0, mod0:mod0 + 1, :]
    gain = g_ref[gidx:gidx + 1, :] * (1.0 + m_ref[0, mod0 + 1:mod0 + 2, :])
    return (_rms(h) * gain + shift).astype(BF16)


def _mod_row_map(tm, rows_per_mod, fixed_row):
    if fixed_row is not None:
        return lambda i, *_: (fixed_row, 0, 0)
    tiles_per_mod = rows_per_mod // tm
    return lambda i, *_: (i // tiles_per_mod, 0, 0)


def _prenorm_kernel(h_ref, m_ref, g_ref, u_ref, *, mod0, gidx):
    u_ref[...] = _mod_norm(h_ref[...], m_ref, g_ref, mod0, gidx)


def _prenorm(h, m3, norm_g, *, mod0, gidx, tm, rows_per_mod, fixed_row=None):
    t = h.shape[0]
    return pl.pallas_call(
        functools.partial(_prenorm_kernel, mod0=mod0, gidx=gidx),
        grid=(t // tm,),
        in_specs=[
            pl.BlockSpec((tm, D_MODEL), lambda i: (i, 0)),
            pl.BlockSpec((1, N_MOD, D_MODEL), _mod_row_map(tm, rows_per_mod, fixed_row)),
            pl.BlockSpec((6, D_MODEL), lambda i: (0, 0)),
        ],
        out_specs=pl.BlockSpec((tm, D_MODEL), lambda i: (i, 0)),
        out_shape=jax.ShapeDtypeStruct((t, D_MODEL), BF16),
        compiler_params=pltpu.CompilerParams(
            dimension_semantics=("parallel",), vmem_limit_bytes=VMEM_LIMIT),
        name="prenorm",
    )(h, m3, norm_g)


def _ffn_up_kernel(*refs, cast_w_out):
    if cast_w_out:
        u_ref, wa_ref, wb_ref, wo_ref, act_ref, wo_bf_ref, wa_bf, wb_bf = refs
        wo_bf_ref[...] = wo_ref[0].astype(BF16)
    else:
        u_ref, wa_ref, wb_ref, act_ref, wa_bf, wb_bf = refs

    @pl.when(pl.program_id(1) == 0)
    def _():
        wa_bf[...] = wa_ref[0].astype(BF16)
        wb_bf[...] = wb_ref[0].astype(BF16)

    for r in range(0, u_ref.shape[0], FFN_UP_ROW_CHUNK):
        u = u_ref[r:r + FFN_UP_ROW_CHUNK, :]
        a = _dot(u, wa_bf[...])
        b = _dot(u, wb_bf[...])
        act_ref[r:r + FFN_UP_ROW_CHUNK, :] = (_silu(b) * a).astype(BF16)


def _ffn_up(u, w_in3, w_out3=None, *, tm, tf):
    t = u.shape[0]
    nf = D_FF // tf
    n_i = t // tm
    in_specs = [
        pl.BlockSpec((tm, D_MODEL), lambda j, i: (i, 0)),
        pl.BlockSpec((1, D_MODEL, tf), lambda j, i: (0, 0, j)),
        pl.BlockSpec((1, D_MODEL, tf), lambda j, i: (0, 0, j + nf)),
    ]
    out_specs = [pl.BlockSpec((tm, tf), lambda j, i: (i, j))]
    out_shape = [jax.ShapeDtypeStruct((t, D_FF), BF16)]
    args = [u, w_in3, w_in3]
    if w_out3 is not None:
        slab = D_FF // (nf * n_i)
        assert slab * nf * n_i == D_FF and slab % 16 == 0, "one bf16-tileable row slab of w_out per grid step"
        in_specs.append(pl.BlockSpec((1, slab, D_MODEL), lambda j, i: (0, j * n_i + i, 0)))
        out_specs.append(pl.BlockSpec((slab, D_MODEL), lambda j, i: (j * n_i + i, 0)))
        out_shape.append(jax.ShapeDtypeStruct((D_FF, D_MODEL), BF16))
        args.append(w_out3)
    return pl.pallas_call(
        functools.partial(_ffn_up_kernel, cast_w_out=w_out3 is not None),
        grid=(nf, n_i),
        in_specs=in_specs,
        out_specs=out_specs,
        out_shape=out_shape,
        scratch_shapes=[pltpu.VMEM((D_MODEL, tf), BF16), pltpu.VMEM((D_MODEL, tf), BF16)],
        compiler_params=pltpu.CompilerParams(
            dimension_semantics=("parallel", "arbitrary"), vmem_limit_bytes=VMEM_LIMIT),
        name="ffn_up",
    )(*args)


def _ffn_down_kernel(*refs, mod0, gidx, next_mod0, next_gidx, row_chunk):
    if next_mod0 is None:
        act_ref, h_ref, m_ref, g_ref, wo_hbm, o_ref, wo_ref, wo_sem = refs
    else:
        act_ref, h_ref, m_ref, g_ref, wo_hbm, wlo_ref, o_ref, u_ref, lo_ref, wo_ref, wo_sem = refs
    slab = D_FF // FFN_DOWN_W_SLABS

    def w_copy(s):
        rows = pl.ds(s * slab, slab)
        return pltpu.make_async_copy(wo_hbm.at[rows, :], wo_ref.at[rows, :], wo_sem.at[s])

    def body(first):
        gain = (MACARON_WEIGHT * m_ref[0, mod0 + 2:mod0 + 3, :]) * g_ref[gidx + 1:gidx + 2, :]
        us = []
        for r in range(0, h_ref.shape[0], row_chunk):
            rows = slice(r, r + row_chunk)
            if first and r == 0:
                y = None
                for s in range(FFN_DOWN_W_SLABS):
                    w_copy(s).wait()
                    ks = slice(s * slab, (s + 1) * slab)
                    part = _dot(act_ref[rows, ks], wo_ref[ks, :])
                    y = part if y is None else y + part
            else:
                y = _dot(act_ref[rows, :], wo_ref[...])
            o = h_ref[rows, :] + _rms(y) * gain
            o_ref[rows, :] = o
            if next_mod0 is not None:
                us.append(_mod_norm(o, m_ref, g_ref, next_mod0, next_gidx))
                u_ref[rows, :] = us[-1]
        if next_mod0 is not None:
            lo_ref[...] = _dot(jnp.concatenate(us, axis=0), wlo_ref[...]).astype(BF16)

    @pl.when(pl.program_id(0) == 0)
    def _():
        for s in range(FFN_DOWN_W_SLABS):
            w_copy(s).start()
        body(True)

    @pl.when(pl.program_id(0) > 0)
    def _():
        body(False)


def _ffn_down(act, h, m3, norm_g, w_out, w_lo, *, mod0, gidx, next_mod0, next_gidx, tm, rows_per_mod,
              fixed_row=None):
    t = h.shape[0]
    emit_next = next_mod0 is not None
    tok = lambda i: (i, 0)
    const = lambda i: (0, 0)
    in_specs = [
        pl.BlockSpec((tm, D_FF), tok),
        pl.BlockSpec((tm, D_MODEL), tok),
        pl.BlockSpec((1, N_MOD, D_MODEL), _mod_row_map(tm, rows_per_mod, fixed_row)),
        pl.BlockSpec((6, D_MODEL), const),
        pl.BlockSpec(memory_space=pl.ANY),
    ]
    out_specs = [pl.BlockSpec((tm, D_MODEL), tok)]
    out_shape = [jax.ShapeDtypeStruct((t, D_MODEL), F32)]
    args = [act, h, m3, norm_g, w_out]
    if emit_next:
        in_specs.append(pl.BlockSpec((D_MODEL, LO_PAD), const))
        out_specs += [pl.BlockSpec((tm, D_MODEL), tok), pl.BlockSpec((tm, LO_PAD), tok)]
        out_shape += [jax.ShapeDtypeStruct((t, D_MODEL), BF16), jax.ShapeDtypeStruct((t, LO_PAD), BF16)]
        args.append(w_lo)
    return pl.pallas_call(
        functools.partial(_ffn_down_kernel, mod0=mod0, gidx=gidx, next_mod0=next_mod0, next_gidx=next_gidx,
                          row_chunk=tm // 2),
        grid=(t // tm,),
        in_specs=in_specs,
        out_specs=out_specs,
        out_shape=out_shape,
        scratch_shapes=[pltpu.VMEM((D_FF, D_MODEL), BF16), pltpu.SemaphoreType.DMA((FFN_DOWN_W_SLABS,))],
        compiler_params=pltpu.CompilerParams(
            dimension_semantics=("arbitrary",), vmem_limit_bytes=BIG_VMEM_LIMIT),
        name="ffn_down",
    )(*args)


def _proj_kernel(*refs, head_blocks, n_side):
    u_ref, w_ref, wg_ref = refs[:3]
    side_in = refs[3:3 + n_side]
    p_ref = refs[3 + n_side]
    side_out = refs[4 + n_side:]
    n = pl.program_id(0)

    for src, dst in zip(side_in, side_out):
        dst[...] = src[0].astype(BF16)

    wsel = jnp.where(n < head_blocks, w_ref[...], wg_ref[...])
    p_ref[...] = _dot(u_ref[...], wsel).astype(BF16)


def _proj(u, w, w_gates, side=(), *, tm, first_block, head_blocks, n_blocks, rot):
    t = u.shape[0]
    tn = PROJ_TN
    n_i = t // tm
    assert not side or n_blocks * n_i >= PROJ_SIDE_SLABS
    slab_idx = lambda n, i: jnp.minimum(n * n_i + i, PROJ_SIDE_SLABS - 1)
    in_specs = [
        pl.BlockSpec((tm, D_MODEL), lambda n, i: (i, 0)),
        pl.BlockSpec((D_MODEL, tn), lambda n, i: (0, first_block + jnp.minimum(n, head_blocks - 1))),
        pl.BlockSpec((D_MODEL, tn), lambda n, i: (0, jnp.maximum(n - head_blocks, 0))),
    ]
    out_specs = [pl.BlockSpec((tm, tn), lambda n, i: (i, (n + rot) % n_blocks))]
    out_shape = [jax.ShapeDtypeStruct((t, n_blocks * tn), BF16)]
    for a in side:
        rows, cols = a.shape[1] // PROJ_SIDE_SLABS, a.shape[2]
        assert rows * PROJ_SIDE_SLABS == a.shape[1] and rows % 16 == 0, "bf16-tileable row slabs"
        in_specs.append(pl.BlockSpec((1, rows, cols), lambda n, i: (0, slab_idx(n, i), 0)))
        out_specs.append(pl.BlockSpec((rows, cols), lambda n, i: (slab_idx(n, i), 0)))
        out_shape.append(jax.ShapeDtypeStruct(a.shape[1:], BF16))
    return pl.pallas_call(
        functools.partial(_proj_kernel, head_blocks=head_blocks, n_side=len(side)),
        grid=(n_blocks, n_i),
        in_specs=in_specs,
        out_specs=out_specs,
        out_shape=out_shape,
        compiler_params=pltpu.CompilerParams(
            dimension_semantics=("arbitrary", "arbitrary"), vmem_limit_bytes=VMEM_LIMIT),
        name="proj",
    )(u, w, w_gates, *side)


def _log2_decay(lo_bf16, gu, bias):
    z = _dot(lo_bf16, gu) + bias
    return (jnp.minimum(z, 0.0) - jnp.log(1.0 + jnp.exp(-jnp.abs(z)))) * (LOG2_E / GATE_TAU)


def _split2(x):
    hi = x.astype(BF16)
    lo = (x - hi.astype(F32)).astype(BF16)
    return jnp.concatenate([hi, lo], axis=0)


def _tri2(n, upper, strict):
    r = lax.broadcasted_iota(jnp.int32, (n, 2 * n), 0)
    c = lax.broadcasted_iota(jnp.int32, (n, 2 * n), 1) & (n - 1)
    if upper:
        m = (c > r) if strict else (c >= r)
    else:
        m = (c < r) if strict else (c <= r)
    return m.astype(BF16)


def _gla_kernel(q_ref, k_ref, v_ref, lo_ref, ck_ref, cv_ref, clo_ref, gu_ref, gb_ref, on_ref,
                qd_f, kd_f, qd_b, kd_b, o_acc, o_bwd, a_f, a_b, st_f, st_b):
    C = GLA_CHUNK
    n_chunks = SEQ // C
    n_pairs = n_chunks // 2
    gu = gu_ref[0]
    bias = gb_ref[0]

    la_c = _log2_decay(clo_ref[0], gu, bias)
    ck = ck_ref[0].astype(F32)
    cv = cv_ref[0]
    dec_f = jnp.exp2(_dot(_tri2(CTX_LEN, True, True), _split2(la_c[:, :HEAD_K])))
    dec_b = jnp.exp2(_dot(_tri2(CTX_LEN, False, True), _split2(la_c[:, HEAD_K:])))
    st_f[...] = _dot_t_lhs(cv, (ck * dec_f).astype(BF16))
    st_b[...] = _dot_t_lhs(cv, (ck * dec_b).astype(BF16))

    tr = lax.broadcasted_iota(jnp.int32, (2 * C, 2 * C), 0)
    tc = lax.broadcasted_iota(jnp.int32, (2 * C, 2 * C), 1) & (C - 1)
    tri_fb = (((tr < C) & (tc <= tr)) | ((tr >= C) & (tc >= tr - C))).astype(BF16)
    pr = lax.broadcasted_iota(jnp.int32, (C, 2 * C), 0)
    pc = lax.broadcasted_iota(jnp.int32, (C, 2 * C), 1)
    keep_f = pc <= pr
    keep_b = pc - C >= pr

    def prepare(i, carry):
        r0 = pl.multiple_of(i * (GLA_PREP_CHUNKS * C), GLA_PREP_CHUNKS * C)
        la = _log2_decay(lo_ref[0, pl.ds(r0, GLA_PREP_CHUNKS * C), :], gu, bias)
        cum = [_dot(tri_fb, _split2(la[j * C:(j + 1) * C, :])) for j in range(GLA_PREP_CHUNKS)]
        chunks = []
        for j in range(GLA_PREP_CHUNKS):
            rows = pl.ds(r0 + j * C, C)
            qc = q_ref[0, rows, :].astype(F32)
            kc = k_ref[0, rows, :].astype(F32)
            b_f = cum[j][:C, :HEAD_K]
            b_b = cum[j][C:, HEAD_K:]
            dirs = []
            for fwd, b in ((True, b_f), (False, b_b)):
                if fwd:
                    b_mid, b_last = b[C // 2 - 1:C // 2, :], b[C - 1:C, :]
                else:
                    b_mid, b_last = b[C // 2:C // 2 + 1, :], b[0:1, :]
                qs = qc * jnp.exp2(b - (b_mid + Q_SCALE_LOG2))
                ks = kc * jnp.exp2(b_mid - b)
                dirs.append(dict(qs=qs.astype(BF16), ks=ks.astype(BF16), qd=qs * jnp.exp2(b_mid),
                                 kd=ks * jnp.exp2(b_last - b_mid), a=jnp.exp2(b_last)))
            scores = _dot_t_rhs(jnp.concatenate([dirs[0]["qs"], dirs[1]["qs"]], axis=0),
                                jnp.concatenate([dirs[0]["ks"], dirs[1]["ks"]], axis=0))
            chunks.append(dict(rows=rows, f=dirs[0], b=dirs[1], scores=scores))
        crosses = []
        for p in range(GLA_PREP_CHUNKS // 2):
            lo_c, hi_c = chunks[2 * p], chunks[2 * p + 1]
            lo_f, hi_f, lo_b, hi_b = lo_c["f"], hi_c["f"], lo_c["b"], hi_c["b"]
            qd_f[lo_c["rows"], :] = lo_f["qd"].astype(BF16)
            qd_f[hi_c["rows"], :] = (hi_f["qd"] * lo_f["a"]).astype(BF16)
            kd_f[lo_c["rows"], :] = (lo_f["kd"] * hi_f["a"]).astype(BF16)
            kd_f[hi_c["rows"], :] = hi_f["kd"].astype(BF16)
            qd_b[hi_c["rows"], :] = hi_b["qd"].astype(BF16)
            qd_b[lo_c["rows"], :] = (lo_b["qd"] * hi_b["a"]).astype(BF16)
            kd_b[hi_c["rows"], :] = (hi_b["kd"] * lo_b["a"]).astype(BF16)
            kd_b[lo_c["rows"], :] = lo_b["kd"].astype(BF16)
            pair = pl.ds(i * (GLA_PREP_CHUNKS // 2) + p, 1)
            a_f[pair, :] = lo_f["a"] * hi_f["a"]
            a_b[pair, :] = lo_b["a"] * hi_b["a"]
            crosses.append(_dot_t_rhs(
                jnp.concatenate([hi_f["qd"].astype(BF16), lo_b["qd"].astype(BF16)], axis=0),
                jnp.concatenate([lo_f["kd"].astype(BF16), hi_b["kd"].astype(BF16)], axis=0)))
        for p in range(GLA_PREP_CHUNKS // 2):
            lo_c, hi_c = chunks[2 * p], chunks[2 * p + 1]
            v_lo = v_ref[0, lo_c["rows"], :]
            v_hi = v_ref[0, hi_c["rows"], :]
            zero = jnp.zeros((C, 2 * C), F32)
            w_lo = jnp.concatenate(
                [jnp.where(keep_f, lo_c["scores"][:C, :], jnp.where(keep_b, lo_c["scores"][C:, :], 0.0)),
                 jnp.where(pc >= C, crosses[p][C:, :], zero)], axis=1)
            o_acc[lo_c["rows"], :] = _dot(w_lo.astype(BF16), jnp.concatenate([v_lo, v_lo, v_lo, v_hi], axis=0))
            w_hi = jnp.concatenate(
                [jnp.where(keep_f, hi_c["scores"][:C, :], jnp.where(keep_b, hi_c["scores"][C:, :], 0.0)),
                 jnp.where(pc < C, crosses[p][:C, :], zero)], axis=1)
            o_acc[hi_c["rows"], :] = _dot(w_hi.astype(BF16), jnp.concatenate([v_hi, v_hi, v_lo, v_lo], axis=0))
        return carry

    def scan(i, carry, final):
        steps = ((i, qd_f, kd_f, a_f, st_f), (n_pairs - 1 - i, qd_b, kd_b, a_b, st_b))
        rows = [pl.ds(pl.multiple_of(c * (2 * C), 2 * C), 2 * C) for c, *_ in steps]
        incs = [_dot_t_lhs(v_ref[0, r, :], kd[r, :]) for r, (_, _, kd, _, _) in zip(rows, steps)]
        outs = [_dot_t_rhs(qd[r, :], st[...].astype(BF16)) for r, (_, qd, _, _, st) in zip(rows, steps)]
        if not final:
            o_acc[rows[0], :] += outs[0]
            upper = pl.ds(pl.multiple_of((n_pairs // 2 - 1 - i) * (2 * C), 2 * C), 2 * C)
            o_bwd[upper, :] = outs[1]
        else:
            upper = pl.ds(pl.multiple_of((i - n_pairs // 2) * (2 * C), 2 * C), 2 * C)
            on_ref[0, rows[0], :] = _rms(o_acc[rows[0], :] + o_bwd[upper, :] + outs[0]).astype(BF16)
            on_ref[0, rows[1], :] = _rms(o_acc[rows[1], :] + outs[1]).astype(BF16)
        for inc, (c, _, _, a, st) in zip(incs, steps):
            st[...] = st[...] * a[pl.ds(c, 1), :] + inc
        return carry

    lax.fori_loop(0, n_chunks // GLA_PREP_CHUNKS, prepare, 0)
    lax.fori_loop(0, n_pairs // 2, functools.partial(scan, final=False), 0, unroll=4)
    lax.fori_loop(n_pairs // 2, n_pairs, functools.partial(scan, final=True), 0, unroll=4)


def _gla(p3, lo3, cp3, clo3, gu, gbias):
    qb, kb, vb = COL_Q // HEAD_K, COL_K // HEAD_K, COL_V // HEAD_V
    n_chunks = SEQ // GLA_CHUNK
    return pl.pallas_call(
        _gla_kernel,
        grid=(BATCH, GLA_HEADS),
        in_specs=[
            pl.BlockSpec((1, SEQ, HEAD_K), lambda b, h: (b, 0, qb + h)),
            pl.BlockSpec((1, SEQ, HEAD_K), lambda b, h: (b, 0, kb + h)),
            pl.BlockSpec((1, SEQ, HEAD_V), lambda b, h: (b, 0, vb + h)),
            pl.BlockSpec((1, SEQ, LO_PAD), lambda b, h: (b, 0, 0)),
            pl.BlockSpec((1, CTX_LEN, HEAD_K), lambda b, h: (b, 0, h)),
            pl.BlockSpec((1, CTX_LEN, HEAD_V), lambda b, h: (b, 0, GLA_DK // HEAD_V + h)),
            pl.BlockSpec((1, CTX_LEN, LO_PAD), lambda b, h: (b, 0, 0)),
            pl.BlockSpec((1, LO_PAD, 2 * HEAD_K), lambda b, h: (h, 0, 0)),
            pl.BlockSpec((1, 1, 2 * HEAD_K), lambda b, h: (h, 0, 0)),
        ],
        out_specs=pl.BlockSpec((1, SEQ, HEAD_V), lambda b, h: (b, 0, h)),
        out_shape=jax.ShapeDtypeStruct((BATCH, SEQ, GLA_DV), BF16),
        scratch_shapes=[
            pltpu.VMEM((SEQ, HEAD_K), BF16),
            pltpu.VMEM((SEQ, HEAD_K), BF16),
            pltpu.VMEM((SEQ, HEAD_K), BF16),
            pltpu.VMEM((SEQ, HEAD_K), BF16),
            pltpu.VMEM((SEQ, HEAD_V), F32),
            pltpu.VMEM((SEQ // 2, HEAD_V), F32),
            pltpu.VMEM((n_chunks // 2, HEAD_K), F32),
            pltpu.VMEM((n_chunks // 2, HEAD_K), F32),
            pltpu.VMEM((HEAD_V, HEAD_K), F32),
            pltpu.VMEM((HEAD_V, HEAD_K), F32),
        ],
        compiler_params=pltpu.CompilerParams(
            dimension_semantics=("parallel", "arbitrary"), vmem_limit_bytes=BIG_VMEM_LIMIT),
        name="gla",
    )(p3, p3, p3, lo3, cp3, cp3, clo3, gu, gbias)


def _mix_kernel(bg_ref, cg_ref, hv_ref, r_ref, ga_ref, gb_ref, on_ref, cw_ref, cb_ref, gng_ref, wc_ref, wg_ref,
                mix_ref, *, row_chunk):
    chunks = [slice(r, r + row_chunk) for r in range(0, mix_ref.shape[0], row_chunk)]
    col = lax.broadcasted_iota(jnp.int32, (row_chunk, 1), 0) & (GRID_W - 1)
    branches = []
    for rows in chunks:
        z = cg_ref[rows, :].astype(F32) * hv_ref[rows, :].astype(F32)
        z_prev = jnp.where(col == 0, 0.0, pltpu.roll(z, 1, 0))
        z_next = jnp.where(col == GRID_W - 1, 0.0, pltpu.roll(z, row_chunk - 1, 0))
        y = z_prev * cw_ref[0:1, :] + z * cw_ref[1:2, :] + z_next * cw_ref[2:3, :] + cb_ref[...]
        y_a = _dot((bg_ref[rows, :].astype(F32) * y).astype(BF16), wc_ref[...])
        rd = on_ref[rows, :].astype(F32) * gng_ref[...] * _silu(r_ref[rows, :].astype(F32))
        branches.append((y_a, _dot(rd.astype(BF16), wg_ref[...])))
    for rows, (y_a, y_b) in zip(chunks, branches):
        mix_ref[rows, :] = (jax.nn.sigmoid(ga_ref[rows, :].astype(F32)) * y_a
                            + jax.nn.sigmoid(gb_ref[rows, :].astype(F32)) * y_b).astype(BF16)


def _mix_out_kernel(mix_ref, x_ref, m_ref, g_ref, wo_ref, o_ref, u_ref, *, row_chunk):
    gain = m_ref[0, 5:6, :] * g_ref[3:4, :]
    for r in range(0, x_ref.shape[0], row_chunk):
        rows = slice(r, r + row_chunk)
        o = x_ref[rows, :] + _rms(_dot(mix_ref[rows, :], wo_ref[...])) * gain
        o_ref[rows, :] = o
        u_ref[rows, :] = _mod_norm(o, m_ref, g_ref, 6, 4)


def _merge(p, on, x1, m3, norm_g, conv_w, conv_b, gng, wc, wg, wo, *, tm):
    t = x1.shape[0]
    tiles_per_mod = SEQ // tm
    tok = lambda blk: (lambda i: (i, blk))
    const2 = lambda i: (0, 0)
    mix = pl.pallas_call(
        functools.partial(_mix_kernel, row_chunk=tm // 2),
        grid=(t // tm,),
        in_specs=[
            pl.BlockSpec((tm, CONV_W), tok(COL_BG // CONV_W)),
            pl.BlockSpec((tm, CONV_W), tok(COL_CG // CONV_W)),
            pl.BlockSpec((tm, CONV_W), tok(COL_HV // CONV_W)),
            pl.BlockSpec((tm, GLA_DV), tok(COL_R // GLA_DV)),
            pl.BlockSpec((tm, D_MODEL), tok(COL_GA // D_MODEL)),
            pl.BlockSpec((tm, D_MODEL), tok(COL_GB // D_MODEL)),
            pl.BlockSpec((tm, GLA_DV), tok(0)),
            pl.BlockSpec((3, CONV_W), const2),
            pl.BlockSpec((1, CONV_W), const2),
            pl.BlockSpec((1, GLA_DV), const2),
            pl.BlockSpec((CONV_W, D_MODEL), const2),
            pl.BlockSpec((GLA_DV, D_MODEL), const2),
        ],
        out_specs=pl.BlockSpec((tm, D_MODEL), tok(0)),
        out_shape=jax.ShapeDtypeStruct((t, D_MODEL), BF16),
        compiler_params=pltpu.CompilerParams(
            dimension_semantics=("parallel",), vmem_limit_bytes=VMEM_LIMIT),
        name="mix",
    )(p, p, p, p, p, p, on, conv_w, conv_b, gng, wc, wg)
    return pl.pallas_call(
        functools.partial(_mix_out_kernel, row_chunk=tm // 2),
        grid=(t // tm,),
        in_specs=[
            pl.BlockSpec((tm, D_MODEL), tok(0)),
            pl.BlockSpec((tm, D_MODEL), tok(0)),
            pl.BlockSpec((1, N_MOD, D_MODEL), lambda i: (i // tiles_per_mod, 0, 0)),
            pl.BlockSpec((6, D_MODEL), const2),
            pl.BlockSpec((D_MODEL, D_MODEL), const2),
        ],
        out_specs=[pl.BlockSpec((tm, D_MODEL), tok(0)), pl.BlockSpec((tm, D_MODEL), tok(0))],
        out_shape=[jax.ShapeDtypeStruct((t, D_MODEL), F32), jax.ShapeDtypeStruct((t, D_MODEL), BF16)],
        compiler_params=pltpu.CompilerParams(
            dimension_semantics=("parallel",), vmem_limit_bytes=VMEM_LIMIT),
        name="mix_out",
    )(mix, x1, m3, norm_g, wo)


def kernel(x, c, ctx, c_ctx, w_mod, b_mod, norm_g, ffn1_w_in, ffn1_w_out, w_in, conv_w, conv_b,
           conv_out, gate_up, gate_bias, gla_norm_g, gla_out, w_o, ffn2_w_in, ffn2_w_out):
    assert w_mod.shape[0] == 1, "single-layer block"
    t = BATCH * SEQ
    tc = BATCH * CTX_LEN

    lo0 = PROJ_HEAD_BLOCKS * PROJ_TN
    w = w_in[0].astype(BF16)
    w_gates = w[:, lo0 + 2 * GATE_RANK:]
    w_lo = jnp.pad(w[:, lo0:lo0 + 2 * GATE_RANK], ((0, 0), (0, LO_PAD - 2 * GATE_RANK)))
    per_head = lambda a: a.reshape(GATE_RANK, GLA_HEADS, HEAD_K).transpose(1, 0, 2)
    gu = jnp.zeros((GLA_HEADS, LO_PAD, 2 * HEAD_K), F32)
    gu = gu.at[:, 0:GATE_RANK, :HEAD_K].set(per_head(gate_up[0, 0]))
    gu = gu.at[:, GATE_RANK:2 * GATE_RANK, HEAD_K:].set(per_head(gate_up[0, 1]))
    gu = gu.astype(BF16)
    gbias = jnp.concatenate([gate_bias[0, 0].reshape(GLA_HEADS, 1, HEAD_K),
                             gate_bias[0, 1].reshape(GLA_HEADS, 1, HEAD_K)], axis=2)
    g = norm_g[0]

    cc = jnp.concatenate([c, c_ctx[None, :], jnp.zeros((MOD_ROWS - BATCH - 1, D_MODEL), F32)], axis=0)
    m3 = _modulation(cc, w_mod[0], b_mod).reshape(MOD_ROWS, N_MOD, D_MODEL)

    xf = x.reshape(t, D_MODEL)
    cf = ctx.reshape(tc, D_MODEL)

    u1 = _prenorm(xf, m3, g, mod0=0, gidx=0, tm=PRENORM_TM, rows_per_mod=SEQ)
    act1, f1_out = _ffn_up(u1, ffn1_w_in, ffn1_w_out, tm=FFN_UP_TM, tf=FFN_TF)
    x1, u2, lo = _ffn_down(act1, xf, m3, g, f1_out, w_lo, mod0=0, gidx=0, next_mod0=3, next_gidx=2,
                           tm=FFN_DOWN_TM, rows_per_mod=SEQ)
    cu1 = _prenorm(cf, m3, g, mod0=0, gidx=0, tm=CTX_PRENORM_TM, rows_per_mod=CTX_LEN, fixed_row=CTX_ROW)
    cact = _ffn_up(cu1, ffn1_w_in, tm=CTX_FFN_UP_TM, tf=FFN_TF)[0]
    _, cu2, clo = _ffn_down(cact, cf, m3, g, f1_out, w_lo, mod0=0, gidx=0, next_mod0=3, next_gidx=2,
                            tm=CTX_FFN_DOWN_TM, rows_per_mod=CTX_LEN, fixed_row=CTX_ROW)

    p, wc, wg, wo = _proj(u2, w, w_gates, (conv_out, gla_out, w_o), tm=PROJ_TM, first_block=0,
                          head_blocks=PROJ_HEAD_BLOCKS, n_blocks=PROJ_BLOCKS, rot=PROJ_ROT)
    cp = _proj(cu2, w, w_gates, tm=CTX_PROJ_TM, first_block=PROJ_K_BLOCK, head_blocks=3, n_blocks=3, rot=0)[0]

    on = _gla(p.reshape(BATCH, SEQ, P_COLS), lo.reshape(BATCH, SEQ, LO_PAD),
              cp.reshape(BATCH, CTX_LEN, GLA_DK + GLA_DV), clo.reshape(BATCH, CTX_LEN, LO_PAD),
              gu, gbias)

    x2, u3 = _merge(p, on.reshape(t, GLA_DV), x1, m3, g, conv_w[0], conv_b, gla_norm_g, wc, wg, wo, tm=MERGE_TM)
    act2, f2_out = _ffn_up(u3, ffn2_w_in, ffn2_w_out, tm=FFN_UP_TM, tf=FFN_TF)
    out = _ffn_down(act2, x2, m3, g, f2_out, None, mod0=6, gidx=4, next_mod0=None, next_gidx=None,
                    tm=FFN_DOWN_TM, rows_per_mod=SEQ)[0]
    return out.reshape(BATCH, SEQ, D_MODEL)
```

```python
import functools

import jax
import jax.numpy as jnp
from jax import lax
from jax.experimental import pallas as pl
from jax.experimental.pallas import tpu as pltpu

F32 = jnp.float32
BF16 = jnp.bfloat16

D_MODEL = 2048
BATCH = 4
SEQ = 4096
GRID_W = 64
CTX_LEN = 256
D_FF = 5632
MACARON_WEIGHT = 0.5
CONV_W = 1024
GLA_HEADS = 4
GLA_DK = 1024
GLA_DV = 2048
HEAD_K = GLA_DK // GLA_HEADS
HEAD_V = GLA_DV // GLA_HEADS
GATE_RANK = 16
GATE_TAU = 16.0
N_MOD = 9
EPS = 1e-6
LOG2_E = 1.4426950408889634
Q_SCALE_LOG2 = 4.0
assert 2.0 ** (2 * Q_SCALE_LOG2) == HEAD_K

LANES = 128
MOD_ROWS = 8
CTX_ROW = BATCH
GLA_CHUNK = 64
GLA_PREP_CHUNKS = 8
FFN_UP_ROW_CHUNK = 256

PRENORM_TM, CTX_PRENORM_TM = 2048, 1024
FFN_UP_TM, CTX_FFN_UP_TM, FFN_TF = 2048, 1024, 512
FFN_DOWN_TM, CTX_FFN_DOWN_TM = 512, 512
PROJ_TM, CTX_PROJ_TM = 2048, 1024
MERGE_TM = 512
MOD_TN = 1024
LO_PAD = LANES

PROJ_TN = 1024
PROJ_HEAD_BLOCKS = 9
PROJ_BLOCKS = 13
PROJ_ROT = 8
PROJ_K_BLOCK = 4
PROJ_SIDE_SLABS = 64
COL_V, COL_R, COL_GA, COL_GB = 0, 2048, 4096, 6144
COL_BG, COL_CG, COL_HV, COL_Q, COL_K = 8192, 9216, 10240, 11264, 12288
P_COLS = PROJ_BLOCKS * PROJ_TN

VMEM_LIMIT = 56 * 1024 * 1024
BIG_VMEM_LIMIT = 60 * 1024 * 1024


def _dot(a, b):
    return jnp.dot(a, b, preferred_element_type=F32)


def _dot_t_lhs(a, b):
    return lax.dot_general(a, b, (((0,), (0,)), ((), ())), preferred_element_type=F32)


def _dot_t_rhs(a, b):
    return lax.dot_general(a, b, (((1,), (1,)), ((), ())), preferred_element_type=F32)


def _rms(x):
    return x * lax.rsqrt(jnp.mean(x * x, axis=-1, keepdims=True) + EPS)


def _silu(x):
    return x * jax.nn.sigmoid(x)


def _mod_kernel(c_ref, w_ref, b_ref, o_ref):
    s = _silu(c_ref[...]).astype(BF16)
    o_ref[...] = _dot(s, w_ref[...].astype(BF16)) + b_ref[...]


def _modulation(cc, w_mod, b_mod):
    n = w_mod.shape[1]
    tn = MOD_TN
    return pl.pallas_call(
        _mod_kernel,
        grid=(n // tn,),
        in_specs=[
            pl.BlockSpec((MOD_ROWS, D_MODEL), lambda j: (0, 0)),
            pl.BlockSpec((D_MODEL, tn), lambda j: (0, j)),
            pl.BlockSpec((1, tn), lambda j: (0, j)),
        ],
        out_specs=pl.BlockSpec((MOD_ROWS, tn), lambda j: (0, j)),
        out_shape=jax.ShapeDtypeStruct((MOD_ROWS, n), F32),
        compiler_params=pltpu.CompilerParams(
            dimension_semantics=("arbitrary",), vmem_limit_bytes=VMEM_LIMIT),
        name="modulation",
    )(cc, w_mod, b_mod)


def _mod_norm(h, m_ref, g_ref, mod0, gidx):
    shift = m_ref[0, mod0:mod0 + 1, :]
    gain = g_ref[gidx:gidx + 1, :] * (1.0 + m_ref[0, mod0 + 1:mod0 + 2, :])
    return (_rms(h) * gain + shift).astype(BF16)


def _mod_row_map(tm, rows_per_mod, fixed_row):
    if fixed_row is not None:
        return lambda i, *_: (fixed_row, 0, 0)
    tiles_per_mod = rows_per_mod // tm
    return lambda i, *_: (i // tiles_per_mod, 0, 0)


def _prenorm_kernel(h_ref, m_ref, g_ref, u_ref, *, mod0, gidx):
    u_ref[...] = _mod_norm(h_ref[...], m_ref, g_ref, mod0, gidx)


def _prenorm(h, m3, norm_g, *, mod0, gidx, tm, rows_per_mod, fixed_row=None):
    t = h.shape[0]
    return pl.pallas_call(
        functools.partial(_prenorm_kernel, mod0=mod0, gidx=gidx),
        grid=(t // tm,),
        in_specs=[
            pl.BlockSpec((tm, D_MODEL), lambda i: (i, 0)),
            pl.BlockSpec((1, N_MOD, D_MODEL), _mod_row_map(tm, rows_per_mod, fixed_row)),
            pl.BlockSpec((6, D_MODEL), lambda i: (0, 0)),
        ],
        out_specs=pl.BlockSpec((tm, D_MODEL), lambda i: (i, 0)),
        out_shape=jax.ShapeDtypeStruct((t, D_MODEL), BF16),
        compiler_params=pltpu.CompilerParams(
            dimension_semantics=("parallel",), vmem_limit_bytes=VMEM_LIMIT),
        name="prenorm",
    )(h, m3, norm_g)


def _ffn_up_kernel(*refs, cast_w_out):
    if cast_w_out:
        u_ref, wa_ref, wb_ref, wo_ref, act_ref, wo_bf_ref, wa_bf, wb_bf = refs
        wo_bf_ref[...] = wo_ref[0].astype(BF16)
    else:
        u_ref, wa_ref, wb_ref, act_ref, wa_bf, wb_bf = refs

    @pl.when(pl.program_id(1) == 0)
    def _():
        wa_bf[...] = wa_ref[0].astype(BF16)
        wb_bf[...] = wb_ref[0].astype(BF16)

    for r in range(0, u_ref.shape[0], FFN_UP_ROW_CHUNK):
        u = u_ref[r:r + FFN_UP_ROW_CHUNK, :]
        a = _dot(u, wa_bf[...])
        b = _dot(u, wb_bf[...])
        act_ref[r:r + FFN_UP_ROW_CHUNK, :] = (_silu(b) * a).astype(BF16)


def _ffn_up(u, w_in3, w_out3=None, *, tm, tf):
    t = u.shape[0]
    nf = D_FF // tf
    n_i = t // tm
    in_specs = [
        pl.BlockSpec((tm, D_MODEL), lambda j, i: (i, 0)),
        pl.BlockSpec((1, D_MODEL, tf), lambda j, i: (0, 0, j)),
        pl.BlockSpec((1, D_MODEL, tf), lambda j, i: (0, 0, j + nf)),
    ]
    out_specs = [pl.BlockSpec((tm, tf), lambda j, i: (i, j))]
    out_shape = [jax.ShapeDtypeStruct((t, D_FF), BF16)]
    args = [u, w_in3, w_in3]
    if w_out3 is not None:
        slab = D_FF // (nf * n_i)
        assert slab * nf * n_i == D_FF and slab % 16 == 0, "one bf16-tileable row slab of w_out per grid step"
        in_specs.append(pl.BlockSpec((1, slab, D_MODEL), lambda j, i: (0, j * n_i + i, 0)))
        out_specs.append(pl.BlockSpec((slab, D_MODEL), lambda j, i: (j * n_i + i, 0)))
        out_shape.append(jax.ShapeDtypeStruct((D_FF, D_MODEL), BF16))
        args.append(w_out3)
    return pl.pallas_call(
        functools.partial(_ffn_up_kernel, cast_w_out=w_out3 is not None),
        grid=(nf, n_i),
        in_specs=in_specs,
        out_specs=out_specs,
        out_shape=out_shape,
        scratch_shapes=[pltpu.VMEM((D_MODEL, tf), BF16), pltpu.VMEM((D_MODEL, tf), BF16)],
        compiler_params=pltpu.CompilerParams(
            dimension_semantics=("parallel", "arbitrary"), vmem_limit_bytes=VMEM_LIMIT),
        name="ffn_up",
    )(*args)


def _ffn_down_kernel(*refs, mod0, gidx, next_mod0, next_gidx, row_chunk):
    if next_mod0 is None:
        act_ref, h_ref, m_ref, g_ref, wo_ref, o_ref = refs
    else:
        act_ref, h_ref, m_ref, g_ref, wo_ref, wlo_ref, o_ref, u_ref, lo_ref = refs
    gain = (MACARON_WEIGHT * m_ref[0, mod0 + 2:mod0 + 3, :]) * g_ref[gidx + 1:gidx + 2, :]
    us = []
    for r in range(0, h_ref.shape[0], row_chunk):
        rows = slice(r, r + row_chunk)
        y = _dot(act_ref[rows, :], wo_ref[...])
        o = h_ref[rows, :] + _rms(y) * gain
        o_ref[rows, :] = o
        if next_mod0 is not None:
            us.append(_mod_norm(o, m_ref, g_ref, next_mod0, next_gidx))
            u_ref[rows, :] = us[-1]
    if next_mod0 is not None:
        lo_ref[...] = _dot(jnp.concatenate(us, axis=0), wlo_ref[...]).astype(BF16)


def _ffn_down(act, h, m3, norm_g, w_out, w_lo, *, mod0, gidx, next_mod0, next_gidx, tm, rows_per_mod,
              fixed_row=None):
    t = h.shape[0]
    emit_next = next_mod0 is not None
    tok = lambda i: (i, 0)
    const = lambda i: (0, 0)
    in_specs = [
        pl.BlockSpec((tm, D_FF), tok),
        pl.BlockSpec((tm, D_MODEL), tok),
        pl.BlockSpec((1, N_MOD, D_MODEL), _mod_row_map(tm, rows_per_mod, fixed_row)),
        pl.BlockSpec((6, D_MODEL), const),
        pl.BlockSpec((D_FF, D_MODEL), const),
    ]
    out_specs = [pl.BlockSpec((tm, D_MODEL), tok)]
    out_shape = [jax.ShapeDtypeStruct((t, D_MODEL), F32)]
    args = [act, h, m3, norm_g, w_out]
    if emit_next:
        in_specs.append(pl.BlockSpec((D_MODEL, LO_PAD), const))
        out_specs += [pl.BlockSpec((tm, D_MODEL), tok), pl.BlockSpec((tm, LO_PAD), tok)]
        out_shape += [jax.ShapeDtypeStruct((t, D_MODEL), BF16), jax.ShapeDtypeStruct((t, LO_PAD), BF16)]
        args.append(w_lo)
    return pl.pallas_call(
        functools.partial(_ffn_down_kernel, mod0=mod0, gidx=gidx, next_mod0=next_mod0, next_gidx=next_gidx,
                          row_chunk=tm // 2),
        grid=(t // tm,),
        in_specs=in_specs,
        out_specs=out_specs,
        out_shape=out_shape,
        compiler_params=pltpu.CompilerParams(
            dimension_semantics=("parallel",), vmem_limit_bytes=BIG_VMEM_LIMIT),
        name="ffn_down",
    )(*args)


def _proj_kernel(*refs, head_blocks, n_side):
    u_ref, w_ref, wg_ref = refs[:3]
    side_in = refs[3:3 + n_side]
    p_ref = refs[3 + n_side]
    side_out = refs[4 + n_side:]
    n = pl.program_id(0)

    for src, dst in zip(side_in, side_out):
        dst[...] = src[0].astype(BF16)

    wsel = jnp.where(n < head_blocks, w_ref[...], wg_ref[...])
    p_ref[...] = _dot(u_ref[...], wsel).astype(BF16)


def _proj(u, w, w_gates, side=(), *, tm, first_block, head_blocks, n_blocks, rot):
    t = u.shape[0]
    tn = PROJ_TN
    n_i = t // tm
    assert not side or n_blocks * n_i >= PROJ_SIDE_SLABS
    slab_idx = lambda n, i: jnp.minimum(n * n_i + i, PROJ_SIDE_SLABS - 1)
    in_specs = [
        pl.BlockSpec((tm, D_MODEL), lambda n, i: (i, 0)),
        pl.BlockSpec((D_MODEL, tn), lambda n, i: (0, first_block + jnp.minimum(n, head_blocks - 1))),
        pl.BlockSpec((D_MODEL, tn), lambda n, i: (0, jnp.maximum(n - head_blocks, 0))),
    ]
    out_specs = [pl.BlockSpec((tm, tn), lambda n, i: (i, (n + rot) % n_blocks))]
    out_shape = [jax.ShapeDtypeStruct((t, n_blocks * tn), BF16)]
    for a in side:
        rows, cols = a.shape[1] // PROJ_SIDE_SLABS, a.shape[2]
        assert rows * PROJ_SIDE_SLABS == a.shape[1] and rows % 16 == 0, "bf16-tileable row slabs"
        in_specs.append(pl.BlockSpec((1, rows, cols), lambda n, i: (0, slab_idx(n, i), 0)))
        out_specs.append(pl.BlockSpec((rows, cols), lambda n, i: (slab_idx(n, i), 0)))
        out_shape.append(jax.ShapeDtypeStruct(a.shape[1:], BF16))
    return pl.pallas_call(
        functools.partial(_proj_kernel, head_blocks=head_blocks, n_side=len(side)),
        grid=(n_blocks, n_i),
        in_specs=in_specs,
        out_specs=out_specs,
        out_shape=out_shape,
        compiler_params=pltpu.CompilerParams(
            dimension_semantics=("arbitrary", "arbitrary"), vmem_limit_bytes=VMEM_LIMIT),
        name="proj",
    )(u, w, w_gates, *side)


def _log2_decay(lo_bf16, gu, bias):
    z = _dot(lo_bf16, gu) + bias
    return (jnp.minimum(z, 0.0) - jnp.log(1.0 + jnp.exp(-jnp.abs(z)))) * (LOG2_E / GATE_TAU)


def _split2(x):
    hi = x.astype(BF16)
    lo = (x - hi.astype(F32)).astype(BF16)
    return jnp.concatenate([hi, lo], axis=0)


def _tri2(n, upper, strict):
    r = lax.broadcasted_iota(jnp.int32, (n, 2 * n), 0)
    c = lax.broadcasted_iota(jnp.int32, (n, 2 * n), 1) & (n - 1)
    if upper:
        m = (c > r) if strict else (c >= r)
    else:
        m = (c < r) if strict else (c <= r)
    return m.astype(BF16)


def _gla_kernel(q_ref, k_ref, v_ref, lo_ref, ck_ref, cv_ref, clo_ref, gu_ref, gb_ref, on_ref,
                qd_f, kd_f, qd_b, kd_b, o_acc, o_bwd, a_f, a_b, st_f, st_b):
    C = GLA_CHUNK
    n_chunks = SEQ // C
    n_pairs = n_chunks // 2
    gu = gu_ref[0]
    bias = gb_ref[0]

    la_c = _log2_decay(clo_ref[0], gu, bias)
    ck = ck_ref[0].astype(F32)
    cv = cv_ref[0]
    dec_f = jnp.exp2(_dot(_tri2(CTX_LEN, True, True), _split2(la_c[:, :HEAD_K])))
    dec_b = jnp.exp2(_dot(_tri2(CTX_LEN, False, True), _split2(la_c[:, HEAD_K:])))
    st_f[...] = _dot_t_lhs(cv, (ck * dec_f).astype(BF16))
    st_b[...] = _dot_t_lhs(cv, (ck * dec_b).astype(BF16))

    tr = lax.broadcasted_iota(jnp.int32, (2 * C, 2 * C), 0)
    tc = lax.broadcasted_iota(jnp.int32, (2 * C, 2 * C), 1) & (C - 1)
    tri_fb = (((tr < C) & (tc <= tr)) | ((tr >= C) & (tc >= tr - C))).astype(BF16)
    pr = lax.broadcasted_iota(jnp.int32, (C, 2 * C), 0)
    pc = lax.broadcasted_iota(jnp.int32, (C, 2 * C), 1)
    keep_f = pc <= pr
    keep_b = pc - C >= pr

    def prepare(i, carry):
        r0 = pl.multiple_of(i * (GLA_PREP_CHUNKS * C), GLA_PREP_CHUNKS * C)
        la = _log2_decay(lo_ref[0, pl.ds(r0, GLA_PREP_CHUNKS * C), :], gu, bias)
        cum = [_dot(tri_fb, _split2(la[j * C:(j + 1) * C, :])) for j in range(GLA_PREP_CHUNKS)]
        chunks = []
        for j in range(GLA_PREP_CHUNKS):
            rows = pl.ds(r0 + j * C, C)
            qc = q_ref[0, rows, :].astype(F32)
            kc = k_ref[0, rows, :].astype(F32)
            b_f = cum[j][:C, :HEAD_K]
            b_b = cum[j][C:, HEAD_K:]
            dirs = []
            for fwd, b in ((True, b_f), (False, b_b)):
                if fwd:
                    b_mid, b_last = b[C // 2 - 1:C // 2, :], b[C - 1:C, :]
                else:
                    b_mid, b_last = b[C // 2:C // 2 + 1, :], b[0:1, :]
                qs = qc * jnp.exp2(b - (b_mid + Q_SCALE_LOG2))
                ks = kc * jnp.exp2(b_mid - b)
                dirs.append(dict(qs=qs.astype(BF16), ks=ks.astype(BF16), qd=qs * jnp.exp2(b_mid),
                                 kd=ks * jnp.exp2(b_last - b_mid), a=jnp.exp2(b_last)))
            scores = _dot_t_rhs(jnp.concatenate([dirs[0]["qs"], dirs[1]["qs"]], axis=0),
                                jnp.concatenate([dirs[0]["ks"], dirs[1]["ks"]], axis=0))
            chunks.append(dict(rows=rows, f=dirs[0], b=dirs[1], scores=scores))
        crosses = []
        for p in range(GLA_PREP_CHUNKS // 2):
            lo_c, hi_c = chunks[2 * p], chunks[2 * p + 1]
            lo_f, hi_f, lo_b, hi_b = lo_c["f"], hi_c["f"], lo_c["b"], hi_c["b"]
            qd_f[lo_c["rows"], :] = lo_f["qd"].astype(BF16)
            qd_f[hi_c["rows"], :] = (hi_f["qd"] * lo_f["a"]).astype(BF16)
            kd_f[lo_c["rows"], :] = (lo_f["kd"] * hi_f["a"]).astype(BF16)
            kd_f[hi_c["rows"], :] = hi_f["kd"].astype(BF16)
            qd_b[hi_c["rows"], :] = hi_b["qd"].astype(BF16)
            qd_b[lo_c["rows"], :] = (lo_b["qd"] * hi_b["a"]).astype(BF16)
            kd_b[hi_c["rows"], :] = (hi_b["kd"] * lo_b["a"]).astype(BF16)
            kd_b[lo_c["rows"], :] = lo_b["kd"].astype(BF16)
            pair = pl.ds(i * (GLA_PREP_CHUNKS // 2) + p, 1)
            a_f[pair, :] = lo_f["a"] * hi_f["a"]
            a_b[pair, :] = lo_b["a"] * hi_b["a"]
            crosses.append(_dot_t_rhs(
                jnp.concatenate([hi_f["qd"].astype(BF16), lo_b["qd"].astype(BF16)], axis=0),
                jnp.concatenate([lo_f["kd"].astype(BF16), hi_b["kd"].astype(BF16)], axis=0)))
        for p in range(GLA_PREP_CHUNKS // 2):
            lo_c, hi_c = chunks[2 * p], chunks[2 * p + 1]
            v_lo = v_ref[0, lo_c["rows"], :]
            v_hi = v_ref[0, hi_c["rows"], :]
            zero = jnp.zeros((C, 2 * C), F32)
            w_lo = jnp.concatenate(
                [jnp.where(keep_f, lo_c["scores"][:C, :], jnp.where(keep_b, lo_c["scores"][C:, :], 0.0)),
                 jnp.where(pc >= C, crosses[p][C:, :], zero)], axis=1)
            o_acc[lo_c["rows"], :] = _dot(w_lo.astype(BF16), jnp.concatenate([v_lo, v_lo, v_lo, v_hi], axis=0))
            w_hi = jnp.concatenate(
                [jnp.where(keep_f, hi_c["scores"][:C, :], jnp.where(keep_b, hi_c["scores"][C:, :], 0.0)),
                 jnp.where(pc < C, crosses[p][:C, :], zero)], axis=1)
            o_acc[hi_c["rows"], :] = _dot(w_hi.astype(BF16), jnp.concatenate([v_hi, v_hi, v_lo, v_lo], axis=0))
        return carry

    def scan(i, carry, final):
        steps = ((i, qd_f, kd_f, a_f, st_f), (n_pairs - 1 - i, qd_b, kd_b, a_b, st_b))
        rows = [pl.ds(pl.multiple_of(c * (2 * C), 2 * C), 2 * C) for c, *_ in steps]
        incs = [_dot_t_lhs(v_ref[0, r, :], kd[r, :]) for r, (_, _, kd, _, _) in zip(rows, steps)]
        outs = [_dot_t_rhs(qd[r, :], st[...].astype(BF16)) for r, (_, qd, _, _, st) in zip(rows, steps)]
        if not final:
            o_acc[rows[0], :] += outs[0]
            upper = pl.ds(pl.multiple_of((n_pairs // 2 - 1 - i) * (2 * C), 2 * C), 2 * C)
            o_bwd[upper, :] = outs[1]
        else:
            upper = pl.ds(pl.multiple_of((i - n_pairs // 2) * (2 * C), 2 * C), 2 * C)
            on_ref[0, rows[0], :] = _rms(o_acc[rows[0], :] + o_bwd[upper, :] + outs[0]).astype(BF16)
            on_ref[0, rows[1], :] = _rms(o_acc[rows[1], :] + outs[1]).astype(BF16)
        for inc, (c, _, _, a, st) in zip(incs, steps):
            st[...] = st[...] * a[pl.ds(c, 1), :] + inc
        return carry

    lax.fori_loop(0, n_chunks // GLA_PREP_CHUNKS, prepare, 0)
    lax.fori_loop(0, n_pairs // 2, functools.partial(scan, final=False), 0, unroll=8)
    lax.fori_loop(n_pairs // 2, n_pairs, functools.partial(scan, final=True), 0, unroll=8)


def _gla(p3, lo3, cp3, clo3, gu, gbias):
    qb, kb, vb = COL_Q // HEAD_K, COL_K // HEAD_K, COL_V // HEAD_V
    n_chunks = SEQ // GLA_CHUNK
    return pl.pallas_call(
        _gla_kernel,
        grid=(BATCH, GLA_HEADS),
        in_specs=[
            pl.BlockSpec((1, SEQ, HEAD_K), lambda b, h: (b, 0, qb + h)),
            pl.BlockSpec((1, SEQ, HEAD_K), lambda b, h: (b, 0, kb + h)),
            pl.BlockSpec((1, SEQ, HEAD_V), lambda b, h: (b, 0, vb + h)),
            pl.BlockSpec((1, SEQ, LO_PAD), lambda b, h: (b, 0, 0)),
            pl.BlockSpec((1, CTX_LEN, HEAD_K), lambda b, h: (b, 0, h)),
            pl.BlockSpec((1, CTX_LEN, HEAD_V), lambda b, h: (b, 0, GLA_DK // HEAD_V + h)),
            pl.BlockSpec((1, CTX_LEN, LO_PAD), lambda b, h: (b, 0, 0)),
            pl.BlockSpec((1, LO_PAD, 2 * HEAD_K), lambda b, h: (h, 0, 0)),
            pl.BlockSpec((1, 1, 2 * HEAD_K), lambda b, h: (h, 0, 0)),
        ],
        out_specs=pl.BlockSpec((1, SEQ, HEAD_V), lambda b, h: (b, 0, h)),
        out_shape=jax.ShapeDtypeStruct((BATCH, SEQ, GLA_DV), BF16),
        scratch_shapes=[
            pltpu.VMEM((SEQ, HEAD_K), BF16),
            pltpu.VMEM((SEQ, HEAD_K), BF16),
            pltpu.VMEM((SEQ, HEAD_K), BF16),
            pltpu.VMEM((SEQ, HEAD_K), BF16),
            pltpu.VMEM((SEQ, HEAD_V), F32),
            pltpu.VMEM((SEQ // 2, HEAD_V), F32),
            pltpu.VMEM((n_chunks // 2, HEAD_K), F32),
            pltpu.VMEM((n_chunks // 2, HEAD_K), F32),
            pltpu.VMEM((HEAD_V, HEAD_K), F32),
            pltpu.VMEM((HEAD_V, HEAD_K), F32),
        ],
        compiler_params=pltpu.CompilerParams(
            dimension_semantics=("parallel", "arbitrary"), vmem_limit_bytes=BIG_VMEM_LIMIT),
        name="gla",
    )(p3, p3, p3, lo3, cp3, cp3, clo3, gu, gbias)


def _mix_kernel(bg_ref, cg_ref, hv_ref, r_ref, ga_ref, gb_ref, on_ref, cw_ref, cb_ref, gng_ref, wc_ref, wg_ref,
                mix_ref, *, row_chunk):
    chunks = [slice(r, r + row_chunk) for r in range(0, mix_ref.shape[0], row_chunk)]
    col = lax.broadcasted_iota(jnp.int32, (row_chunk, 1), 0) & (GRID_W - 1)
    branches = []
    for rows in chunks:
        z = cg_ref[rows, :].astype(F32) * hv_ref[rows, :].astype(F32)
        z_prev = jnp.where(col == 0, 0.0, pltpu.roll(z, 1, 0))
        z_next = jnp.where(col == GRID_W - 1, 0.0, pltpu.roll(z, row_chunk - 1, 0))
        y = z_prev * cw_ref[0:1, :] + z * cw_ref[1:2, :] + z_next * cw_ref[2:3, :] + cb_ref[...]
        y_a = _dot((bg_ref[rows, :].astype(F32) * y).astype(BF16), wc_ref[...])
        rd = on_ref[rows, :].astype(F32) * gng_ref[...] * _silu(r_ref[rows, :].astype(F32))
        branches.append((y_a, _dot(rd.astype(BF16), wg_ref[...])))
    for rows, (y_a, y_b) in zip(chunks, branches):
        mix_ref[rows, :] = (jax.nn.sigmoid(ga_ref[rows, :].astype(F32)) * y_a
                            + jax.nn.sigmoid(gb_ref[rows, :].astype(F32)) * y_b).astype(BF16)


def _mix_out_kernel(mix_ref, x_ref, m_ref, g_ref, wo_ref, o_ref, u_ref, *, row_chunk):
    gain = m_ref[0, 5:6, :] * g_ref[3:4, :]
    for r in range(0, x_ref.shape[0], row_chunk):
        rows = slice(r, r + row_chunk)
        o = x_ref[rows, :] + _rms(_dot(mix_ref[rows, :], wo_ref[...])) * gain
        o_ref[rows, :] = o
        u_ref[rows, :] = _mod_norm(o, m_ref, g_ref, 6, 4)


def _merge(p, on, x1, m3, norm_g, conv_w, conv_b, gng, wc, wg, wo, *, tm):
    t = x1.shape[0]
    tiles_per_mod = SEQ // tm
    tok = lambda blk: (lambda i: (i, blk))
    const2 = lambda i: (0, 0)
    mix = pl.pallas_call(
        functools.partial(_mix_kernel, row_chunk=tm // 2),
        grid=(t // tm,),
        in_specs=[
            pl.BlockSpec((tm, CONV_W), tok(COL_BG // CONV_W)),
            pl.BlockSpec((tm, CONV_W), tok(COL_CG // CONV_W)),
            pl.BlockSpec((tm, CONV_W), tok(COL_HV // CONV_W)),
            pl.BlockSpec((tm, GLA_DV), tok(COL_R // GLA_DV)),
            pl.BlockSpec((tm, D_MODEL), tok(COL_GA // D_MODEL)),
            pl.BlockSpec((tm, D_MODEL), tok(COL_GB // D_MODEL)),
            pl.BlockSpec((tm, GLA_DV), tok(0)),
            pl.BlockSpec((3, CONV_W), const2),
            pl.BlockSpec((1, CONV_W), const2),
            pl.BlockSpec((1, GLA_DV), const2),
            pl.BlockSpec((CONV_W, D_MODEL), const2),
            pl.BlockSpec((GLA_DV, D_MODEL), const2),
        ],
        out_specs=pl.BlockSpec((tm, D_MODEL), tok(0)),
        out_shape=jax.ShapeDtypeStruct((t, D_MODEL), BF16),
        compiler_params=pltpu.CompilerParams(
            dimension_semantics=("parallel",), vmem_limit_bytes=VMEM_LIMIT),
        name="mix",
    )(p, p, p, p, p, p, on, conv_w, conv_b, gng, wc, wg)
    return pl.pallas_call(
        functools.partial(_mix_out_kernel, row_chunk=tm // 2),
        grid=(t // tm,),
        in_specs=[
            pl.BlockSpec((tm, D_MODEL), tok(0)),
            pl.BlockSpec((tm, D_MODEL), tok(0)),
            pl.BlockSpec((1, N_MOD, D_MODEL), lambda i: (i // tiles_per_mod, 0, 0)),
            pl.BlockSpec((6, D_MODEL), const2),
            pl.BlockSpec((D_MODEL, D_MODEL), const2),
        ],
        out_specs=[pl.BlockSpec((tm, D_MODEL), tok(0)), pl.BlockSpec((tm, D_MODEL), tok(0))],
        out_shape=[jax.ShapeDtypeStruct((t, D_MODEL), F32), jax.ShapeDtypeStruct((t, D_MODEL), BF16)],
        compiler_params=pltpu.CompilerParams(
            dimension_semantics=("parallel",), vmem_limit_bytes=VMEM_LIMIT),
        name="mix_out",
    )(mix, x1, m3, norm_g, wo)


def kernel(x, c, ctx, c_ctx, w_mod, b_mod, norm_g, ffn1_w_in, ffn1_w_out, w_in, conv_w, conv_b,
           conv_out, gate_up, gate_bias, gla_norm_g, gla_out, w_o, ffn2_w_in, ffn2_w_out):
    assert w_mod.shape[0] == 1, "single-layer block"
    t = BATCH * SEQ
    tc = BATCH * CTX_LEN

    lo0 = PROJ_HEAD_BLOCKS * PROJ_TN
    w = w_in[0].astype(BF16)
    w_gates = w[:, lo0 + 2 * GATE_RANK:]
    w_lo = jnp.pad(w[:, lo0:lo0 + 2 * GATE_RANK], ((0, 0), (0, LO_PAD - 2 * GATE_RANK)))
    per_head = lambda a: a.reshape(GATE_RANK, GLA_HEADS, HEAD_K).transpose(1, 0, 2)
    gu = jnp.zeros((GLA_HEADS, LO_PAD, 2 * HEAD_K), F32)
    gu = gu.at[:, 0:GATE_RANK, :HEAD_K].set(per_head(gate_up[0, 0]))
    gu = gu.at[:, GATE_RANK:2 * GATE_RANK, HEAD_K:].set(per_head(gate_up[0, 1]))
    gu = gu.astype(BF16)
    gbias = jnp.concatenate([gate_bias[0, 0].reshape(GLA_HEADS, 1, HEAD_K),
                             gate_bias[0, 1].reshape(GLA_HEADS, 1, HEAD_K)], axis=2)
    g = norm_g[0]

    cc = jnp.concatenate([c, c_ctx[None, :], jnp.zeros((MOD_ROWS - BATCH - 1, D_MODEL), F32)], axis=0)
    m3 = _modulation(cc, w_mod[0], b_mod).reshape(MOD_ROWS, N_MOD, D_MODEL)

    xf = x.reshape(t, D_MODEL)
    cf = ctx.reshape(tc, D_MODEL)

    u1 = _prenorm(xf, m3, g, mod0=0, gidx=0, tm=PRENORM_TM, rows_per_mod=SEQ)
    act1, f1_out = _ffn_up(u1, ffn1_w_in, ffn1_w_out, tm=FFN_UP_TM, tf=FFN_TF)
    x1, u2, lo = _ffn_down(act1, xf, m3, g, f1_out, w_lo, mod0=0, gidx=0, next_mod0=3, next_gidx=2,
                           tm=FFN_DOWN_TM, rows_per_mod=SEQ)
    cu1 = _prenorm(cf, m3, g, mod0=0, gidx=0, tm=CTX_PRENORM_TM, rows_per_mod=CTX_LEN, fixed_row=CTX_ROW)
    cact = _ffn_up(cu1, ffn1_w_in, tm=CTX_FFN_UP_TM, tf=FFN_TF)[0]
    _, cu2, clo = _ffn_down(cact, cf, m3, g, f1_out, w_lo, mod0=0, gidx=0, next_mod0=3, next_gidx=2,
                            tm=CTX_FFN_DOWN_TM, rows_per_mod=CTX_LEN, fixed_row=CTX_ROW)

    p, wc, wg, wo = _proj(u2, w, w_gates, (conv_out, gla_out, w_o), tm=PROJ_TM, first_block=0,
                          head_blocks=PROJ_HEAD_BLOCKS, n_blocks=PROJ_BLOCKS, rot=PROJ_ROT)
    cp = _proj(cu2, w, w_gates, tm=CTX_PROJ_TM, first_block=PROJ_K_BLOCK, head_blocks=3, n_blocks=3, rot=0)[0]

    on = _gla(p.reshape(BATCH, SEQ, P_COLS), lo.reshape(BATCH, SEQ, LO_PAD),
              cp.reshape(BATCH, CTX_LEN, GLA_DK + GLA_DV), clo.reshape(BATCH, CTX_LEN, LO_PAD),
              gu, gbias)

    x2, u3 = _merge(p, on.reshape(t, GLA_DV), x1, m3, g, conv_w[0], conv_b, gla_norm_g, wc, wg, wo, tm=MERGE_TM)
    act2, f2_out = _ffn_up(u3, ffn2_w_in, ffn2_w_out, tm=FFN_UP_TM, tf=FFN_TF)
    out = _ffn_down(act2, x2, m3, g, f2_out, None, mod0=6, gidx=4, next_mod0=None, next_gidx=None,
                    tm=FFN_DOWN_TM, rows_per_mod=SEQ)[0]
    return out.reshape(BATCH, SEQ, D_MODEL)
```
